```python
import jax, jax.numpy as jnp
from jax import lax
import numpy as np

D_MODEL = 1024
BATCH = 8
SEQ = 8192
DEPTH = 4
DEC_BATCH = 8
DEC_SEQ = 64
PAST_LEN = 2048

CHUNK = 64
MIX_W = D_MODEL
W_A = MIX_W // 2
W_B = MIX_W - W_A
HD_A = 64
N_HEADS_A = W_A // HD_A
N_KV_A = 2
N_IDX = 4
IDX_DIM = 64
MAX_TOPK = 256
HD_B = 128
N_HEADS_B = W_B // HD_B
Q_BLOCK = 128
ROPE_THETA = 10000.0
EPS = 1e-6
IDX_SCALE = (IDX_DIM ** -0.5) * (N_IDX ** -0.5)
IN_SIZES = (W_A, N_KV_A * HD_A, N_KV_A * HD_A, W_A,
            N_IDX * IDX_DIM, IDX_DIM, N_IDX,
            W_B, W_B, W_B, W_B)
IN_W = sum(IN_SIZES)

kernel_name = 'hymba_dsa_retention_stream_step'


def split_cols(p):
    out = []
    o = 0
    for n in IN_SIZES:
        out.append(p[..., o:o + n])
        o += n
    return out


def rms_norm(x, g):
    xf = x.astype(jnp.float32)
    y = xf * lax.rsqrt(jnp.mean(xf * xf, axis=-1, keepdims=True) + EPS)
    return (y * g.astype(jnp.float32)).astype(x.dtype)


def rope(x, pos):
    d = x.shape[-1]
    half = d // 2
    inv = ROPE_THETA ** (-jnp.arange(half, dtype=jnp.float32) * 2.0 / d)
    ang = pos.astype(jnp.float32)[:, None] * inv[None, :]
    cos = jnp.cos(ang)[None, :, None, :]
    sin = jnp.sin(ang)[None, :, None, :]
    xf = x.astype(jnp.float32)
    x1, x2 = xf[..., :half], xf[..., half:]
    return jnp.concatenate([x1 * cos - x2 * sin, x2 * cos + x1 * sin], axis=-1).astype(x.dtype)


def chunk_limit(pos):
    return (pos // CHUNK + 1) * CHUNK


def dsa_attention(q, qi, wi, pos, k, v, ki):
    B, T = q.shape[0], q.shape[1]
    S = k.shape[1]
    topk = min(MAX_TOPK, S // 4)
    qb = min(Q_BLOCK, T)
    nb = T // qb
    key_pos = jnp.arange(S, dtype=jnp.int32)

    def blocks(a):
        return jnp.moveaxis(a.reshape((B, nb, qb) + a.shape[2:]), 1, 0)

    def one_block(args):
        q_b, qi_b, wi_b, pos_b = args
        lim = chunk_limit(pos_b)
        rel = jax.nn.relu(jnp.einsum('bqhd,bsd->bqhs', qi_b, ki).astype(jnp.float32))
        score = jnp.einsum('bqhs,bqh->bqs', rel, wi_b.astype(jnp.float32)) * IDX_SCALE
        score = jnp.where(key_pos[None, None, :] < lim[None, :, None], score, -jnp.inf)
        _, idx = lax.top_k(score, topk)
        valid = idx < lim[None, :, None]
        kg = jax.vmap(lambda a, i: a[i])(k, idx)
        vg = jax.vmap(lambda a, i: a[i])(v, idx)
        qg = q_b.reshape(B, qb, N_KV_A, N_HEADS_A // N_KV_A, HD_A)
        logits = jnp.einsum('bqngd,bqknd->bqngk', qg, kg).astype(jnp.float32) * (HD_A ** -0.5)
        logits = jnp.where(valid[:, :, None, None, :], logits, -jnp.inf)
        p = jax.nn.softmax(logits, axis=-1).astype(v.dtype)
        o = jnp.einsum('bqngk,bqknd->bqngd', p, vg)
        return o.reshape(B, qb, N_HEADS_A * HD_A)

    out = lax.map(one_block, (blocks(q), blocks(qi), blocks(wi), pos.reshape(nb, qb)))
    return jnp.moveaxis(out, 0, 1).reshape(B, T, N_HEADS_A * HD_A)


def retention(q, k, v, s0):
    B, T, H, d = q.shape
    c = min(CHUNK, T)
    nc = T // c
    lg = jnp.log1p(-jnp.exp2(-5.0 - jnp.arange(H, dtype=jnp.float32)))
    i = jnp.arange(c, dtype=jnp.float32)
    diff = i[:, None] - i[None, :]
    dmask = jnp.exp(jnp.where(diff >= 0, diff[None] * lg[:, None, None], -jnp.inf))
    q_dec = jnp.exp((i + 1.0)[:, None] * lg[None, :])
    k_dec = jnp.exp((c - 1.0 - i)[:, None] * lg[None, :])
    s_dec = jnp.exp(c * lg)

    def step(s, blk):
        qc, kc, vc = blk
        att = jnp.einsum('bihd,bjhd->bhij', qc, kc) * dmask[None]
        o = (jnp.einsum('bhij,bjhe->bihe', att, vc)
             + jnp.einsum('bihd,bhde->bihe', qc, s) * q_dec[None, :, :, None])
        s = s * s_dec[None, :, None, None] + jnp.einsum('bjhd,bjhe->bhde', kc * k_dec[None, :, :, None], vc)
        return s, o

    def blocks(a):
        return jnp.moveaxis(a.astype(jnp.float32).reshape(B, nc, c, H, d), 1, 0)

    s, o = lax.scan(step, s0.astype(jnp.float32), (blocks(q), blocks(k), blocks(v)))
    return jnp.moveaxis(o, 0, 1).reshape(B, T, H, d), s


def head_group_norm(o):
    mu = jnp.mean(o, axis=-1, keepdims=True)
    var = jnp.mean(jnp.square(o - mu), axis=-1, keepdims=True)
    return (o - mu) * lax.rsqrt(var + EPS)


def trunk_layer(x, c, pos, w_ada, b_ada, g_pre, w_in, w_out, g_post, past):
    B, T, _ = x.shape
    mod = jax.nn.silu(c) @ w_ada + b_ada
    shift, scale, gate = jnp.split(mod, 3, axis=-1)
    h = rms_norm(x, g_pre) * (1.0 + scale[:, None, :]) + shift[:, None, :]
    q_a, k_a, v_a, gate_a, q_i, k_i, w_i, q_b, k_b, v_b, gate_b = split_cols(h @ w_in)
    q_a = rope(q_a.reshape(B, T, N_HEADS_A, HD_A), pos)
    k_a = rope(k_a.reshape(B, T, N_KV_A, HD_A), pos)
    v_a = v_a.reshape(B, T, N_KV_A, HD_A)
    q_i = rope(q_i.reshape(B, T, N_IDX, IDX_DIM), pos)
    k_i = rope(k_i.reshape(B, T, 1, IDX_DIM), pos)[:, :, 0]
    if past is None:
        k_all, v_all, ki_all = k_a, v_a, k_i
        s0 = jnp.zeros((B, N_HEADS_B, HD_B, HD_B), jnp.float32)
    else:
        k_all = jnp.concatenate([past[0], k_a], axis=1)
        v_all = jnp.concatenate([past[1], v_a], axis=1)
        ki_all = jnp.concatenate([past[2], k_i], axis=1)
        s0 = past[3]
    o_a = dsa_attention(q_a, q_i, w_i, pos, k_all, v_all, ki_all)
    q_b = rope(q_b.reshape(B, T, N_HEADS_B, HD_B), pos)
    k_b = rope(k_b.reshape(B, T, N_HEADS_B, HD_B), pos) * (HD_B ** -0.5)
    v_b = v_b.reshape(B, T, N_HEADS_B, HD_B)
    o_b, s_new = retention(q_b, k_b, v_b, s0)
    o_b = head_group_norm(o_b).reshape(B, T, W_B).astype(x.dtype)
    mixed = jnp.concatenate([o_a * jax.nn.silu(gate_a), o_b * jax.nn.silu(gate_b)], axis=-1)
    y = rms_norm(mixed @ w_out, g_post)
    x = x + gate[:, None, :] * y
    return x, (k_a, v_a, k_i, s_new.astype(x.dtype))


def setup_inputs(seed: int = 0) -> dict:
    key = jax.random.key(seed)
    ks = jax.random.split(key, 16)
    f32 = jnp.float32
    D = D_MODEL
    return {
        'x_prompt': jax.random.normal(ks[0], (BATCH, SEQ, D), f32),
        'x_sample': jax.random.normal(ks[1], (DEC_BATCH, DEC_SEQ, D), f32),
        'cache_k_a': jax.random.normal(ks[2], (DEPTH, DEC_BATCH, PAST_LEN, N_KV_A, HD_A), f32),
        'cache_v_a': jax.random.normal(ks[3], (DEPTH, DEC_BATCH, PAST_LEN, N_KV_A, HD_A), f32),
        'cache_kidx_a': jax.random.normal(ks[4], (DEPTH, DEC_BATCH, PAST_LEN, IDX_DIM), f32),
        'state_s_b': 0.5 * jax.random.normal(ks[5], (DEPTH, DEC_BATCH, N_HEADS_B, HD_B, HD_B), f32),
        'c_prompt': jax.random.normal(ks[6], (BATCH, D), f32),
        'c_sample': jax.random.normal(ks[7], (DEC_BATCH, D), f32),
        'w_ada': 0.5 * (D ** -0.5) * jax.random.normal(ks[8], (DEPTH, D, 3 * D), f32),
        'b_ada': 0.02 * jax.random.normal(ks[9], (DEPTH, 3 * D), f32),
        'g_pre': 1.0 + 0.05 * jax.random.normal(ks[10], (DEPTH, D), f32),
        'w_in': (D ** -0.5) * jax.random.normal(ks[11], (DEPTH, D, IN_W), f32),
        'w_out': (MIX_W ** -0.5) * jax.random.normal(ks[12], (DEPTH, MIX_W, D), f32),
        'g_post': 1.0 + 0.05 * jax.random.normal(ks[13], (DEPTH, D), f32),
    }


def reference(x_prompt, x_sample, cache_k_a, cache_v_a, cache_kidx_a, state_s_b,
              c_prompt, c_sample, w_ada, b_ada, g_pre, w_in, w_out, g_post):
    past_len = cache_k_a.shape[2]
    pos_p = jnp.arange(x_prompt.shape[1], dtype=jnp.int32)
    pos_s = past_len + jnp.arange(x_sample.shape[1], dtype=jnp.int32)
    hp, hs = x_prompt, x_sample
    kp, vp, kip, sp = [], [], [], []
    ksm, vsm, kism, ssm = [], [], [], []
    for l in range(DEPTH):
        hp, (k_n, v_n, ki_n, s_n) = trunk_layer(hp, c_prompt, pos_p, w_ada[l], b_ada[l], g_pre[l],
                                                w_in[l], w_out[l], g_post[l], None)
        kp.append(k_n); vp.append(v_n); kip.append(ki_n); sp.append(s_n)
        hs, (k_n, v_n, ki_n, s_n) = trunk_layer(hs, c_sample, pos_s, w_ada[l], b_ada[l], g_pre[l],
                                                w_in[l], w_out[l], g_post[l],
                                                (cache_k_a[l], cache_v_a[l], cache_kidx_a[l], state_s_b[l]))
        ksm.append(k_n); vsm.append(v_n); kism.append(ki_n); ssm.append(s_n)
    return (hp, hs,
            jnp.stack(kp), jnp.stack(vp), jnp.stack(kip), jnp.stack(sp),
            jnp.stack(ksm), jnp.stack(vsm), jnp.stack(kism), jnp.stack(ssm))
```

```python
import functools

import jax
import jax.numpy as jnp
from jax import lax
from jax.experimental import pallas as pl
from jax.experimental.pallas import tpu as pltpu

D_MODEL = 1024
CHUNK = 64
W_A = 512
HD_A = 64
N_HEADS_A = 8
N_KV_A = 2
N_GROUP_A = N_HEADS_A // N_KV_A
N_IDX = 4
IDX_DIM = 64
MAX_TOPK = 256
HD_B = 128
N_HEADS_B = 4
W_B = 512
ROPE_THETA = 10000.0
EPS = 1e-6
IDX_SCALE = (IDX_DIM ** -0.5) * (N_IDX ** -0.5)

LANES = 128
VMEM_LIMIT = 56 * 1024 * 1024
INT_MIN = -(2 ** 31)
NEG = -1e30

F32 = jnp.float32
BF16 = jnp.bfloat16

_SEGS = (
    ("qa", N_HEADS_A * LANES, "a"),
    ("ka", N_KV_A * HD_A, "a"),
    ("va", N_KV_A * HD_A, None),
    ("ga", W_A, None),
    ("qi", N_IDX * LANES, "a"),
    ("kiw", LANES, "k"),
    ("qb", W_B, "b"),
    ("kb", W_B, "b"),
    ("vb", W_B, None),
    ("gb", W_B, None),
)
_W_COLS = sum(s[1] for s in _SEGS)
_R_COLS = sum(s[1] for s in _SEGS if s[2] is not None)
_OUT_DTYPES = {"qa": BF16, "ka": F32, "va": F32, "ga": BF16, "qi": BF16, "kiw": F32,
               "qb": BF16, "kb": BF16, "vb": BF16, "gb": BF16}


def _silu(x):
    return x / (1.0 + jnp.exp(-x))


def _mod_kernel(c_ref, w_ref, b_ref, o_ref):
    c = c_ref[...]
    s = _silu(c)
    o_ref[0] = jnp.dot(s.astype(BF16), w_ref[0].astype(BF16), preferred_element_type=F32) + b_ref[0]


def _modulation(c_all, w_ada, b_ada):
    depth = w_ada.shape[0]
    n = c_all.shape[0]
    nblk = 3
    return pl.pallas_call(
        _mod_kernel,
        grid=(depth, nblk),
        in_specs=[
            pl.BlockSpec((n, D_MODEL), lambda l, j: (0, 0)),
            pl.BlockSpec((1, D_MODEL, D_MODEL), lambda l, j: (l, 0, j)),
            pl.BlockSpec((1, 1, D_MODEL), lambda l, j: (l, 0, j)),
        ],
        out_specs=pl.BlockSpec((1, n, D_MODEL), lambda l, j: (l, 0, j)),
        out_shape=jax.ShapeDtypeStruct((depth, n, 3 * D_MODEL), F32),
        compiler_params=pltpu.CompilerParams(dimension_semantics=("arbitrary", "arbitrary"),
                                             vmem_limit_bytes=VMEM_LIMIT),
        name="modulation",
    )(c_all, w_ada, b_ada.reshape(depth, 1, 3 * D_MODEL))


def _inproj_kernel(x_ref, sc_ref, sh_ref, g_ref, w_ref, wr_ref,
                   ca_ref, sa_ref, ck_ref, sk_ref, cb_ref, sb_ref, *out_refs):
    x = x_ref[0]
    ms = jnp.mean(x * x, axis=-1, keepdims=True)
    y = x * lax.rsqrt(ms + EPS) * g_ref[...]
    h = y * sc_ref[0] + sh_ref[0]
    hb = h.astype(BF16)
    tabs = {"a": (ca_ref, sa_ref), "k": (ck_ref, sk_ref), "b": (cb_ref, sb_ref)}
    c0 = 0
    r0 = 0
    for (name, width, kind), o_ref in zip(_SEGS, out_refs):
        step = 2 * LANES if width % (2 * LANES) == 0 else LANES
        for off in range(0, width, step):
            p = jnp.dot(hb, w_ref[:, c0 + off:c0 + off + step], preferred_element_type=F32)
            if kind is not None:
                r = jnp.dot(hb, wr_ref[:, r0 + off:r0 + off + step], preferred_element_type=F32)
                cos = jnp.tile(tabs[kind][0][...], (1, step // LANES))
                sin = jnp.tile(tabs[kind][1][...], (1, step // LANES))
                p = p * cos + r * sin
            if name == "kb":
                p = p * (HD_B ** -0.5)
            if name in ("ga", "gb"):
                p = _silu(p)
            o_ref[0, :, off:off + step] = p.astype(o_ref.dtype)
        c0 += width
        if kind is not None:
            r0 += width


def _rot_cols(w, d):
    k = w.shape[0]
    w3 = w.reshape(k, -1, d)
    return jnp.concatenate([-w3[..., d // 2:], w3[..., :d // 2]], axis=-1).reshape(k, -1)


def _widen_heads(w, nkv_of_head):
    k = w.shape[0]
    nh = w.shape[1] // HD_A
    z = jnp.zeros((k, HD_A), w.dtype)
    cols = []
    for h in range(nh):
        wh = w[:, h * HD_A:(h + 1) * HD_A]
        cols += [wh, z] if nkv_of_head(h) == 0 else [z, wh]
    return jnp.concatenate(cols, axis=1)


def _prep_w_in(w):
    sizes = (W_A, N_KV_A * HD_A, N_KV_A * HD_A, W_A, N_IDX * IDX_DIM, IDX_DIM, N_IDX, W_B, W_B, W_B, W_B)
    parts = []
    o = 0
    for n in sizes:
        parts.append(w[:, o:o + n])
        o += n
    q_a, k_a, v_a, g_a, q_i, k_i, w_i, q_b, k_b, v_b, g_b = parts
    kk = w.shape[0]
    q_a = q_a * (HD_A ** -0.5)
    kiw = jnp.concatenate([k_i, w_i, jnp.zeros((kk, LANES - IDX_DIM - N_IDX), w.dtype)], axis=1)
    kiw_rot = jnp.concatenate([_rot_cols(k_i, IDX_DIM), jnp.zeros((kk, LANES - IDX_DIM), w.dtype)], axis=1)
    qa_head = lambda h: h // N_GROUP_A
    qi_head = lambda h: 0
    segs = {
        "qa": (_widen_heads(q_a, qa_head), _widen_heads(_rot_cols(q_a, HD_A), qa_head)),
        "ka": (k_a, _rot_cols(k_a, HD_A)),
        "va": (v_a, None),
        "ga": (g_a, None),
        "qi": (_widen_heads(q_i, qi_head), _widen_heads(_rot_cols(q_i, IDX_DIM), qi_head)),
        "kiw": (kiw, kiw_rot),
        "qb": (q_b, _rot_cols(q_b, HD_B)),
        "kb": (k_b, _rot_cols(k_b, HD_B)),
        "vb": (v_b, None),
        "gb": (g_b, None),
    }
    wp = jnp.concatenate([segs[s[0]][0] for s in _SEGS], axis=1).astype(BF16)
    wr = jnp.concatenate([segs[s[0]][1] for s in _SEGS if s[2] is not None], axis=1).astype(BF16)
    return wp, wr


def _rope_tables(pos):
    posf = pos.astype(F32)[:, None]

    def tab(d):
        half = d // 2
        inv = ROPE_THETA ** (-jnp.arange(half, dtype=F32) * 2.0 / d)
        ang = posf * inv[None, :]
        reps = LANES // half
        return jnp.tile(jnp.cos(ang), (1, reps)), jnp.tile(jnp.sin(ang), (1, reps))

    ca, sa = tab(HD_A)
    cb, sb = tab(HD_B)
    lane = jnp.arange(LANES)[None, :]
    ck = jnp.where(lane < IDX_DIM, ca, 1.0)
    sk = jnp.where(lane < IDX_DIM, sa, 0.0)
    return ca, sa, ck, sk, cb, sb


def _row_tile(t):
    return min(512, t)


def _inproj(x, scale1p, shift, g_pre, wp, wr, tabs):
    b, t, _ = x.shape
    tm = _row_tile(t)
    nt = t // tm
    const = lambda i, j: (0, 0)
    tab_spec = pl.BlockSpec((tm, LANES), lambda i, j: (j, 0))
    in_specs = [
        pl.BlockSpec((1, tm, D_MODEL), lambda i, j: (i, j, 0)),
        pl.BlockSpec((1, 1, D_MODEL), lambda i, j: (i, 0, 0)),
        pl.BlockSpec((1, 1, D_MODEL), lambda i, j: (i, 0, 0)),
        pl.BlockSpec((1, D_MODEL), const),
        pl.BlockSpec((D_MODEL, _W_COLS), const, pipeline_mode=pl.Buffered(1)),
        pl.BlockSpec((D_MODEL, _R_COLS), const, pipeline_mode=pl.Buffered(1)),
    ] + [tab_spec] * 6
    out_specs = [pl.BlockSpec((1, tm, s[1]), lambda i, j: (i, j, 0)) for s in _SEGS]
    out_shape = [jax.ShapeDtypeStruct((b, t, s[1]), _OUT_DTYPES[s[0]]) for s in _SEGS]
    outs = pl.pallas_call(
        _inproj_kernel,
        grid=(b, nt),
        in_specs=in_specs,
        out_specs=out_specs,
        out_shape=out_shape,
        compiler_params=pltpu.CompilerParams(dimension_semantics=("arbitrary", "arbitrary"),
                                             vmem_limit_bytes=VMEM_LIMIT),
        name="inproj",
    )(x, scale1p, shift, g_pre.reshape(1, D_MODEL), wp, wr, *tabs)
    return dict(zip([s[0] for s in _SEGS], outs))


def _attn_kernel(qa_ref, qi_ref, wq_ref, kt_ref, kit_ref, vaug_ref, tri_ref, o_ref,
                 keys_ref, qst_ref, qist_ref, wb_ref, m_ref, acc_ref, run_ref,
                 *, qb, kb, pos_off, topk):
    j = pl.program_id(1)
    reps = kb // LANES

    row = lax.broadcasted_iota(jnp.int32, (qb, 1), 0)
    pos = pos_off + j * qb + row
    lim = (lax.shift_right_logical(pos, 6) + 1) * CHUNK
    last_pos = pos_off + (j + 1) * qb - 1
    lim_max = (lax.shift_right_logical(last_pos, 6) + 1) * CHUNK
    nkb = lax.div(lim_max + (kb - 1), kb)

    for n in range(N_KV_A):
        for g in range(N_GROUP_A):
            hd = n * N_GROUP_A + g
            qst_ref[n, g * qb:(g + 1) * qb, :] = qa_ref[0, :, hd * LANES:(hd + 1) * LANES]
    for h in range(N_IDX):
        qist_ref[h * qb:(h + 1) * qb, :] = qi_ref[0, :, h * LANES:(h + 1) * LANES]
        wcol = wq_ref[0, :, IDX_DIM + h:IDX_DIM + h + 1] * IDX_SCALE
        wb_ref[h] = jnp.broadcast_to(wcol, (qb, LANES))

    def score_blk(i, carry):
        off = pl.multiple_of(i * kb, kb)
        x = jnp.dot(qist_ref[...], kit_ref[0, :, pl.ds(off, kb)], preferred_element_type=F32)
        sc = None
        for h in range(N_IDX):
            term = jnp.maximum(x[h * qb:(h + 1) * qb], 0.0) * jnp.tile(wb_ref[h], (1, reps))
            sc = term if sc is None else sc + term
        bits = pltpu.bitcast(sc, jnp.int32)
        key = bits ^ (lax.shift_right_arithmetic(bits, 31) & 0x7FFFFFFF)
        kpos = off + lax.broadcasted_iota(jnp.int32, (qb, kb), 1)
        keys_ref[:, pl.ds(off, kb)] = jnp.where(kpos < lim, key, INT_MIN)
        return carry

    lax.fori_loop(0, nkb, score_blk, 0)

    def count_ge(thr):
        thr_t = jnp.tile(thr, (1, reps))

        def blk(i, acc):
            off = pl.multiple_of(i * kb, kb)
            hit = jnp.where(keys_ref[:, pl.ds(off, kb)] >= thr_t, 1, 0)
            for r in range(reps):
                acc = acc + hit[:, r * LANES:(r + 1) * LANES]
            return acc

        acc = lax.fori_loop(0, nkb, blk, jnp.zeros((qb, LANES), jnp.int32))
        return jnp.sum(acc, axis=1, keepdims=True)

    def bis(i, t_u):
        bit = lax.shift_left(jnp.int32(1), 31 - i)
        cand = t_u | bit
        cnt = count_ge(cand ^ INT_MIN)
        return jnp.where(cnt >= topk, cand, t_u)

    t_u = lax.fori_loop(0, 32, bis, jnp.zeros((qb, LANES), jnp.int32))
    thr = jnp.maximum(t_u ^ INT_MIN, INT_MIN + 1)
    n_gt = count_ge(thr + 1)
    need = (topk - n_gt).astype(F32)

    m_ref[...] = jnp.full(m_ref.shape, NEG, F32)
    acc_ref[...] = jnp.zeros(acc_ref.shape, F32)
    run_ref[...] = jnp.zeros(run_ref.shape, F32)
    thr_t = jnp.tile(thr, (1, reps))

    def attn_blk(i, carry):
        off = pl.multiple_of(i * kb, kb)
        key = keys_ref[:, pl.ds(off, kb)]
        eq = key == thr_t
        pre = jnp.dot(jnp.where(eq, 1.0, 0.0).astype(BF16), tri_ref[...], preferred_element_type=F32)
        run = run_ref[...]
        keep_eq = (pre + jnp.tile(run, (1, reps))) <= need
        bias = jnp.where(key > thr_t, 0.0, jnp.where(eq, jnp.where(keep_eq, 0.0, NEG), NEG))
        run_ref[...] = run + jnp.max(pre, axis=1, keepdims=True)
        bias4 = jnp.tile(bias, (N_GROUP_A, 1))
        k_blk = kt_ref[0, :, pl.ds(off, kb)]
        for n in range(N_KV_A):
            s = jnp.dot(qst_ref[n], k_blk, preferred_element_type=F32) + bias4
            m_prev = m_ref[n]
            m_new = jnp.maximum(m_prev, jnp.max(s, axis=1, keepdims=True))
            alpha = jnp.exp(m_prev - m_new)
            p = jnp.exp(s - jnp.tile(m_new, (1, reps)))
            pv = jnp.dot(p.astype(BF16), vaug_ref[0, pl.ds(off, kb), n * LANES:(n + 1) * LANES],
                         preferred_element_type=F32)
            acc_ref[n] = alpha * acc_ref[n] + pv
            m_ref[n] = m_new
        return carry

    lax.fori_loop(0, nkb, attn_blk, 0)

    lane = lax.broadcasted_iota(jnp.int32, (qb, LANES), 1)
    for s_i in range(N_HEADS_A // 2):
        halves = []
        for hd in (2 * s_i, 2 * s_i + 1):
            n, g = hd // N_GROUP_A, hd % N_GROUP_A
            a = acc_ref[n, g * qb:(g + 1) * qb, :]
            halves.append(a / pltpu.roll(a, HD_A, axis=1))
        o_ref[0, :, s_i * LANES:(s_i + 1) * LANES] = jnp.where(
            lane < HD_A, halves[0], pltpu.roll(halves[1], HD_A, axis=1))


def _attention(qa, qi, kiw_q, kt, kit, vaug, pos_off, s_real):
    b, t, _ = qa.shape
    s_pad = kt.shape[2]
    qb = min(128, t)
    kb = 256
    topk = min(MAX_TOPK, s_real // 4)
    tri = (jnp.arange(kb)[:, None] <= jnp.arange(kb)[None, :]).astype(BF16)
    kern = functools.partial(_attn_kernel, qb=qb, kb=kb, pos_off=pos_off, topk=topk)
    return pl.pallas_call(
        kern,
        grid=(b, t // qb),
        in_specs=[
            pl.BlockSpec((1, qb, N_HEADS_A * LANES), lambda i, j: (i, j, 0)),
            pl.BlockSpec((1, qb, N_IDX * LANES), lambda i, j: (i, j, 0)),
            pl.BlockSpec((1, qb, LANES), lambda i, j: (i, j, 0)),
            pl.BlockSpec((1, LANES, s_pad), lambda i, j: (i, 0, 0)),
            pl.BlockSpec((1, LANES, s_pad), lambda i, j: (i, 0, 0)),
            pl.BlockSpec((1, s_pad, N_KV_A * LANES), lambda i, j: (i, 0, 0)),
            pl.BlockSpec((kb, kb), lambda i, j: (0, 0)),
        ],
        out_specs=pl.BlockSpec((1, qb, W_A), lambda i, j: (i, j, 0)),
        out_shape=jax.ShapeDtypeStruct((b, t, W_A), F32),
        scratch_shapes=[
            pltpu.VMEM((qb, s_pad), jnp.int32),
            pltpu.VMEM((N_KV_A, N_GROUP_A * qb, LANES), BF16),
            pltpu.VMEM((N_IDX * qb, LANES), BF16),
            pltpu.VMEM((N_IDX, qb, LANES), F32),
            pltpu.VMEM((N_KV_A, N_GROUP_A * qb, LANES), F32),
            pltpu.VMEM((N_KV_A, N_GROUP_A * qb, LANES), F32),
            pltpu.VMEM((qb, LANES), F32),
        ],
        compiler_params=pltpu.CompilerParams(dimension_semantics=("arbitrary", "arbitrary"),
                                             vmem_limit_bytes=VMEM_LIMIT),
        name="dsa_attention",
    )(qa, qi, kiw_q, kt, kit, vaug, tri)


def _attn_operands(ka, va, kiw, s_pad):
    b, s, _ = ka.shape
    pad = s_pad - s
    kt = jnp.swapaxes(jnp.pad(ka.astype(BF16), ((0, 0), (0, pad), (0, 0))), 1, 2)
    kit = jnp.swapaxes(jnp.pad(kiw.astype(BF16), ((0, 0), (0, pad), (0, 0))), 1, 2)
    vb = jnp.pad(va.astype(BF16), ((0, 0), (0, pad), (0, 0))).reshape(b, s_pad, N_KV_A, HD_A)
    vaug = jnp.concatenate([vb, jnp.ones_like(vb)], axis=-1).reshape(b, s_pad, N_KV_A * LANES)
    return kt, kit, vaug


def _ret_kernel(q_ref, k_ref, v_ref, s0_ref, dm_ref, qd_ref, kd_ref, sd_ref, o_ref, sn_ref, s_scr,
                *, rt, rc):
    t = pl.program_id(1)

    @pl.when(t == 0)
    def _():
        s_scr[...] = s0_ref[0]

    for c in range(rt // rc):
        rows = slice(c * rc, (c + 1) * rc)
        for h in range(N_HEADS_B):
            cols = slice(h * HD_B, (h + 1) * HD_B)
            q = q_ref[0, rows, cols]
            k = k_ref[0, rows, cols]
            v = v_ref[0, rows, cols]
            att = lax.dot_general(q, k, (((1,), (1,)), ((), ())), preferred_element_type=F32) * dm_ref[h]
            s = s_scr[h]
            o = (jnp.dot(att.astype(BF16), v, preferred_element_type=F32)
                 + jnp.dot(q, s.astype(BF16), preferred_element_type=F32) * qd_ref[h])
            kd = (k.astype(F32) * kd_ref[h]).astype(BF16)
            s_scr[h] = s * sd_ref[h] + lax.dot_general(kd, v, (((0,), (0,)), ((), ())),
                                                       preferred_element_type=F32)
            mu = jnp.mean(o, axis=-1, keepdims=True)
            d = o - mu
            var = jnp.mean(d * d, axis=-1, keepdims=True)
            o_ref[0, rows, cols] = d * lax.rsqrt(var + EPS)

    @pl.when(t == pl.num_programs(1) - 1)
    def _():
        sn_ref[0] = s_scr[...]


def _retention(qb_, kb_, vb_, s0):
    b, t, _ = qb_.shape
    rt = min(512, t)
    rc = min(128, t)
    lg = jnp.log1p(-jnp.exp2(-5.0 - jnp.arange(N_HEADS_B, dtype=F32)))
    i = jnp.arange(rc, dtype=F32)
    diff = i[:, None] - i[None, :]
    dmask = jnp.exp(jnp.where(diff >= 0, diff[None] * lg[:, None, None], -jnp.inf))
    ones = jnp.ones((1, 1, HD_B), F32)
    q_dec = jnp.exp((i + 1.0)[None, :] * lg[:, None])[:, :, None] * ones
    k_dec = jnp.exp((rc - 1.0 - i)[None, :] * lg[:, None])[:, :, None] * ones
    s_dec = jnp.exp(rc * lg)[:, None, None] * jnp.ones((1, HD_B, HD_B), F32)
    blk = pl.BlockSpec((1, rt, W_B), lambda i_, j: (i_, j, 0))
    st = pl.BlockSpec((1, N_HEADS_B, HD_B, HD_B), lambda i_, j: (i_, 0, 0, 0))
    c3 = lambda i_, j: (0, 0, 0)
    kern = functools.partial(_ret_kernel, rt=rt, rc=rc)
    return pl.pallas_call(
        kern,
        grid=(b, t // rt),
        in_specs=[blk, blk, blk, st,
                  pl.BlockSpec((N_HEADS_B, rc, rc), c3),
                  pl.BlockSpec((N_HEADS_B, rc, HD_B), c3),
                  pl.BlockSpec((N_HEADS_B, rc, HD_B), c3),
                  pl.BlockSpec((N_HEADS_B, HD_B, HD_B), c3)],
        out_specs=[blk, st],
        out_shape=[jax.ShapeDtypeStruct((b, t, W_B), F32),
                   jax.ShapeDtypeStruct((b, N_HEADS_B, HD_B, HD_B), F32)],
        scratch_shapes=[pltpu.VMEM((N_HEADS_B, HD_B, HD_B), F32)],
        compiler_params=pltpu.CompilerParams(dimension_semantics=("arbitrary", "arbitrary"),
                                             vmem_limit_bytes=VMEM_LIMIT),
        name="retention",
    )(qb_, kb_, vb_, s0, dmask, q_dec, k_dec, s_dec)


def _outproj_kernel(x_ref, oa_ref, ga_ref, ob_ref, gb_ref, w_ref, g_ref, gate_ref, o_ref):
    ma = (oa_ref[0] * ga_ref[0].astype(F32)).astype(BF16)
    mb = (ob_ref[0] * gb_ref[0].astype(F32)).astype(BF16)
    y = (jnp.dot(ma, w_ref[:W_A, :], preferred_element_type=F32)
         + jnp.dot(mb, w_ref[W_A:, :], preferred_element_type=F32))
    ms = jnp.mean(y * y, axis=-1, keepdims=True)
    yn = y * lax.rsqrt(ms + EPS) * g_ref[...]
    o_ref[0] = x_ref[0] + gate_ref[0] * yn


def _outproj(x, oa, ga, ob, gb, w_out, g_post, gate):
    b, t, _ = x.shape
    tm = _row_tile(t)
    row = lambda w: pl.BlockSpec((1, tm, w), lambda i, j: (i, j, 0))
    return pl.pallas_call(
        _outproj_kernel,
        grid=(b, t // tm),
        in_specs=[row(D_MODEL), row(W_A), row(W_A), row(W_B), row(W_B),
                  pl.BlockSpec((D_MODEL, D_MODEL), lambda i, j: (0, 0)),
                  pl.BlockSpec((1, D_MODEL), lambda i, j: (0, 0)),
                  pl.BlockSpec((1, 1, D_MODEL), lambda i, j: (i, 0, 0))],
        out_specs=row(D_MODEL),
        out_shape=jax.ShapeDtypeStruct((b, t, D_MODEL), F32),
        compiler_params=pltpu.CompilerParams(dimension_semantics=("arbitrary", "arbitrary"),
                                             vmem_limit_bytes=VMEM_LIMIT),
        name="outproj",
    )(x, oa, ga, ob, gb, w_out.astype(BF16), g_post.reshape(1, D_MODEL), gate)


def _layer(x, mod, pos_off, tabs, wp, wr, g_pre, w_out, g_post, past):
    b, t, _ = x.shape
    shift, scale, gate = jnp.split(mod, 3, axis=-1)
    p = _inproj(x, (1.0 + scale)[:, None, :], shift[:, None, :], g_pre, wp, wr, tabs)
    ka, va, kiw = p["ka"], p["va"], p["kiw"]
    if past is None:
        k_all, v_all, kiw_all = ka, va, kiw
        s0 = jnp.zeros((b, N_HEADS_B, HD_B, HD_B), F32)
    else:
        pk, pv, pki, s0 = past
        plen = pk.shape[1]
        k_all = jnp.concatenate([pk.reshape(b, plen, N_KV_A * HD_A), ka], axis=1)
        v_all = jnp.concatenate([pv.reshape(b, plen, N_KV_A * HD_A), va], axis=1)
        pki = jnp.pad(pki, ((0, 0), (0, 0), (0, LANES - IDX_DIM)))
        kiw_all = jnp.concatenate([pki, kiw], axis=1)
    s_real = k_all.shape[1]
    s_pad = -(-s_real // 256) * 256
    kt, kit, vaug = _attn_operands(k_all, v_all, kiw_all, s_pad)
    oa = _attention(p["qa"], p["qi"], kiw, kt, kit, vaug, pos_off, s_real)
    ob, s_new = _retention(p["qb"], p["kb"], p["vb"], s0)
    x_new = _outproj(x, oa, p["ga"], ob, p["gb"], w_out, g_post, gate[:, None, :])
    k_n = ka.reshape(b, t, N_KV_A, HD_A)
    v_n = va.reshape(b, t, N_KV_A, HD_A)
    return x_new, (k_n, v_n, kiw[..., :IDX_DIM], s_new)


def kernel(x_prompt, x_sample, cache_k_a, cache_v_a, cache_kidx_a, state_s_b, c_prompt, c_sample,
           w_ada, b_ada, g_pre, w_in, w_out, g_post):
    depth = w_in.shape[0]
    nb = x_prompt.shape[0]
    past_len = cache_k_a.shape[2]
    tabs_p = _rope_tables(jnp.arange(x_prompt.shape[1], dtype=jnp.int32))
    tabs_s = _rope_tables(past_len + jnp.arange(x_sample.shape[1], dtype=jnp.int32))
    mod = _modulation(jnp.concatenate([c_prompt, c_sample], axis=0), w_ada, b_ada)
    hp, hs = x_prompt, x_sample
    outs_p, outs_s = [], []
    for l in range(depth):
        wp, wr = _prep_w_in(w_in[l])
        hp, new_p = _layer(hp, mod[l, :nb], 0, tabs_p, wp, wr, g_pre[l], w_out[l], g_post[l], None)
        outs_p.append(new_p)
        hs, new_s = _layer(hs, mod[l, nb:], past_len, tabs_s, wp, wr, g_pre[l], w_out[l], g_post[l],
                           (cache_k_a[l], cache_v_a[l], cache_kidx_a[l], state_s_b[l]))
        outs_s.append(new_s)
    stack = lambda outs, i: jnp.stack([o[i] for o in outs])
    return (hp, hs,
            stack(outs_p, 0), stack(outs_p, 1), stack(outs_p, 2), stack(outs_p, 3),
            stack(outs_s, 0), stack(outs_s, 1), stack(outs_s, 2), stack(outs_s, 3))
```

```python
import functools
import math

import jax
import jax.numpy as jnp
from jax import lax
from jax.experimental import pallas as pl
from jax.experimental.pallas import tpu as pltpu

D_MODEL = 1024
CHUNK = 64
W_A = 512
HD_A = 64
N_HEADS_A = 8
N_KV_A = 2
N_GROUP_A = N_HEADS_A // N_KV_A
N_IDX = 4
IDX_DIM = 64
MAX_TOPK = 256
HD_B = 128
N_HEADS_B = 4
W_B = 512
ROPE_THETA = 10000.0
EPS = 1e-6
IDX_SCALE = (IDX_DIM ** -0.5) * (N_IDX ** -0.5)
LOG2E = math.log2(math.e)

LANES = 128
PACK16 = 16
VMEM_LIMIT = 56 * 1024 * 1024
INT_MIN = -(2 ** 31)
NEG = -1e30

QB = LANES
SEL_BLK = 512
CNT_BLK = 2 * SEL_BLK
ATT_BLK = 256
VROWS = HD_A + PACK16

F32 = jnp.float32
BF16 = jnp.bfloat16
I16 = jnp.int16
I32 = jnp.int32

_SEGS = (
    ("qa", N_HEADS_A * LANES, "a"),
    ("ka", N_KV_A * HD_A, "a"),
    ("va", N_KV_A * HD_A, None),
    ("ga", W_A, None),
    ("qi", N_IDX * LANES, "a"),
    ("kiw", LANES, "k"),
    ("qb", W_B, "b"),
    ("kb", W_B, "b"),
    ("vb", W_B, None),
    ("gb", W_B, None),
)
_W_COLS = sum(s[1] for s in _SEGS)
_R_COLS = sum(s[1] for s in _SEGS if s[2] is not None)
_OUT_DTYPES = {"qa": BF16, "ka": F32, "va": F32, "ga": BF16, "qi": BF16, "kiw": F32,
               "qb": BF16, "kb": BF16, "vb": BF16, "gb": BF16}


def _silu(x):
    return x / (1.0 + jnp.exp(-x))


def _fold(op, x):
    parts = [x[i] for i in range(x.shape[0])]
    while len(parts) > 1:
        parts = [op(parts[i], parts[i + 1]) for i in range(0, len(parts) - 1, 2)] + parts[len(parts) & ~1:]
    return parts[0]


def _mod_kernel(c_ref, w_ref, b_ref, o_ref):
    c = c_ref[...]
    s = _silu(c)
    o_ref[0] = jnp.dot(s.astype(BF16), w_ref[0].astype(BF16), preferred_element_type=F32) + b_ref[0]


def _modulation(c_all, w_ada, b_ada):
    depth = w_ada.shape[0]
    n = c_all.shape[0]
    nblk = 3
    return pl.pallas_call(
        _mod_kernel,
        grid=(depth, nblk),
        in_specs=[
            pl.BlockSpec((n, D_MODEL), lambda l, j: (0, 0)),
            pl.BlockSpec((1, D_MODEL, D_MODEL), lambda l, j: (l, 0, j)),
            pl.BlockSpec((1, 1, D_MODEL), lambda l, j: (l, 0, j)),
        ],
        out_specs=pl.BlockSpec((1, n, D_MODEL), lambda l, j: (l, 0, j)),
        out_shape=jax.ShapeDtypeStruct((depth, n, 3 * D_MODEL), F32),
        compiler_params=pltpu.CompilerParams(dimension_semantics=("arbitrary", "arbitrary"),
                                             vmem_limit_bytes=VMEM_LIMIT),
        name="modulation",
    )(c_all, w_ada, b_ada.reshape(depth, 1, 3 * D_MODEL))


def _inproj_kernel(x_ref, sc_ref, sh_ref, g_ref, w_ref, wr_ref,
                   ca_ref, sa_ref, ck_ref, sk_ref, cb_ref, sb_ref, *out_refs):
    x = x_ref[0]
    ms = jnp.mean(x * x, axis=-1, keepdims=True)
    y = x * lax.rsqrt(ms + EPS) * g_ref[...]
    h = y * sc_ref[0] + sh_ref[0]
    hb = h.astype(BF16)
    tabs = {"a": (ca_ref, sa_ref), "k": (ck_ref, sk_ref), "b": (cb_ref, sb_ref)}
    c0 = 0
    r0 = 0
    for (name, width, kind), o_ref in zip(_SEGS, out_refs):
        step = 2 * LANES if width % (2 * LANES) == 0 else LANES
        for off in range(0, width, step):
            p = jnp.dot(hb, w_ref[:, c0 + off:c0 + off + step], preferred_element_type=F32)
            if kind is not None:
                r = jnp.dot(hb, wr_ref[:, r0 + off:r0 + off + step], preferred_element_type=F32)
                cos = jnp.tile(tabs[kind][0][...], (1, step // LANES))
                sin = jnp.tile(tabs[kind][1][...], (1, step // LANES))
                p = p * cos + r * sin
            if name == "kb":
                p = p * (HD_B ** -0.5)
            if name in ("ga", "gb"):
                p = _silu(p)
            o_ref[0, :, off:off + step] = p.astype(o_ref.dtype)
        c0 += width
        if kind is not None:
            r0 += width


def _rot_cols(w, d):
    k = w.shape[0]
    w3 = w.reshape(k, -1, d)
    return jnp.concatenate([-w3[..., d // 2:], w3[..., :d // 2]], axis=-1).reshape(k, -1)


def _widen_heads(w, nkv_of_head):
    k = w.shape[0]
    nh = w.shape[1] // HD_A
    z = jnp.zeros((k, HD_A), w.dtype)
    cols = []
    for h in range(nh):
        wh = w[:, h * HD_A:(h + 1) * HD_A]
        cols += [wh, z] if nkv_of_head(h) == 0 else [z, wh]
    return jnp.concatenate(cols, axis=1)


def _prep_w_in(w):
    sizes = (W_A, N_KV_A * HD_A, N_KV_A * HD_A, W_A, N_IDX * IDX_DIM, IDX_DIM, N_IDX, W_B, W_B, W_B, W_B)
    parts = []
    o = 0
    for n in sizes:
        parts.append(w[:, o:o + n])
        o += n
    q_a, k_a, v_a, g_a, q_i, k_i, w_i, q_b, k_b, v_b, g_b = parts
    kk = w.shape[0]
    q_a = q_a * ((HD_A ** -0.5) * LOG2E)
    kiw = jnp.concatenate([k_i, w_i, jnp.zeros((kk, LANES - IDX_DIM - N_IDX), w.dtype)], axis=1)
    kiw_rot = jnp.concatenate([_rot_cols(k_i, IDX_DIM), jnp.zeros((kk, LANES - IDX_DIM), w.dtype)], axis=1)
    qa_head = lambda h: h // N_GROUP_A
    qi_head = lambda h: 0
    segs = {
        "qa": (_widen_heads(q_a, qa_head), _widen_heads(_rot_cols(q_a, HD_A), qa_head)),
        "ka": (k_a, _rot_cols(k_a, HD_A)),
        "va": (v_a, None),
        "ga": (g_a, None),
        "qi": (_widen_heads(q_i, qi_head), _widen_heads(_rot_cols(q_i, IDX_DIM), qi_head)),
        "kiw": (kiw, kiw_rot),
        "qb": (q_b, _rot_cols(q_b, HD_B)),
        "kb": (k_b, _rot_cols(k_b, HD_B)),
        "vb": (v_b, None),
        "gb": (g_b, None),
    }
    wp = jnp.concatenate([segs[s[0]][0] for s in _SEGS], axis=1).astype(BF16)
    wr = jnp.concatenate([segs[s[0]][1] for s in _SEGS if s[2] is not None], axis=1).astype(BF16)
    return wp, wr


def _rope_tables(pos):
    posf = pos.astype(F32)[:, None]

    def tab(d):
        half = d // 2
        inv = ROPE_THETA ** (-jnp.arange(half, dtype=F32) * 2.0 / d)
        ang = posf * inv[None, :]
        reps = LANES // half
        return jnp.tile(jnp.cos(ang), (1, reps)), jnp.tile(jnp.sin(ang), (1, reps))

    ca, sa = tab(HD_A)
    cb, sb = tab(HD_B)
    lane = jnp.arange(LANES)[None, :]
    ck = jnp.where(lane < IDX_DIM, ca, 1.0)
    sk = jnp.where(lane < IDX_DIM, sa, 0.0)
    return ca, sa, ck, sk, cb, sb


def _row_tile(t):
    return min(512, t)


def _inproj(x, scale1p, shift, g_pre, wp, wr, tabs):
    b, t, _ = x.shape
    tm = _row_tile(t)
    nt = t // tm
    const = lambda i, j: (0, 0)
    tab_spec = pl.BlockSpec((tm, LANES), lambda i, j: (j, 0))
    in_specs = [
        pl.BlockSpec((1, tm, D_MODEL), lambda i, j: (i, j, 0)),
        pl.BlockSpec((1, 1, D_MODEL), lambda i, j: (i, 0, 0)),
        pl.BlockSpec((1, 1, D_MODEL), lambda i, j: (i, 0, 0)),
        pl.BlockSpec((1, D_MODEL), const),
        pl.BlockSpec((D_MODEL, _W_COLS), const, pipeline_mode=pl.Buffered(1)),
        pl.BlockSpec((D_MODEL, _R_COLS), const, pipeline_mode=pl.Buffered(1)),
    ] + [tab_spec] * 6
    out_specs = [pl.BlockSpec((1, tm, s[1]), lambda i, j: (i, j, 0)) for s in _SEGS]
    out_shape = [jax.ShapeDtypeStruct((b, t, s[1]), _OUT_DTYPES[s[0]]) for s in _SEGS]
    outs = pl.pallas_call(
        _inproj_kernel,
        grid=(b, nt),
        in_specs=in_specs,
        out_specs=out_specs,
        out_shape=out_shape,
        compiler_params=pltpu.CompilerParams(dimension_semantics=("arbitrary", "arbitrary"),
                                             vmem_limit_bytes=VMEM_LIMIT),
        name="inproj",
    )(x, scale1p, shift, g_pre.reshape(1, D_MODEL), wp, wr, *tabs)
    return dict(zip([s[0] for s in _SEGS], outs))


def _attn_kernel(qa_ref, qi_ref, wq_ref, kk_ref, kiw_ref, vt_ref, tri_ref, o_ref,
                 keys_ref, k16_ref, bias_ref, qt_ref, qit_ref, m_ref, acc_ref, sb_ref, x_ref,
                 *, pos_off, topk, s_real):
    j = pl.program_id(1)
    npair = N_HEADS_A // 2

    qlane = lax.broadcasted_iota(I32, (1, QB), 1)
    pos = pos_off + j * QB + qlane
    lim = jnp.minimum((lax.shift_right_logical(pos, 6) + 1) * CHUNK, s_real)
    last_pos = pos_off + (j + 1) * QB - 1
    lim_max = jnp.minimum((lax.shift_right_logical(last_pos, 6) + 1) * CHUNK, s_real)
    n_sel = lax.div(lim_max + (SEL_BLK - 1), SEL_BLK)
    n_att = n_sel * (SEL_BLK // ATT_BLK)

    for hd in range(N_HEADS_A):
        sl = slice(hd * LANES, (hd + 1) * LANES)
        qt_ref[:, sl] = qa_ref[0, :, sl].astype(F32).T.astype(BF16)
    for h in range(N_IDX):
        sl = slice(h * LANES, (h + 1) * LANES)
        qit_ref[:, sl] = qi_ref[0, :, sl].astype(F32).T.astype(BF16)
    wt = wq_ref[0].T
    w_idx = [wt[IDX_DIM + h:IDX_DIM + h + 1, :] * IDX_SCALE for h in range(N_IDX)]

    def score_dots(off, slot):
        kblk = kiw_ref[0, pl.ds(off, ATT_BLK), :]
        for m in range(N_IDX // 2):
            cols = slice(2 * m * LANES, (2 * m + 2) * LANES)
            x_ref[slot, :, cols] = jnp.dot(kblk, qit_ref[:, cols], preferred_element_type=F32)

    def score_keys(off, slot):
        sc = None
        for h in range(N_IDX):
            term = jnp.maximum(x_ref[slot, :, h * LANES:(h + 1) * LANES], 0.0) * w_idx[h]
            sc = term if sc is None else sc + term
        bits = pltpu.bitcast(sc, I32)
        key = bits ^ (lax.shift_right_arithmetic(bits, 31) & 0x7FFFFFFF)
        kpos = off + lax.broadcasted_iota(I32, (ATT_BLK, QB), 0)
        key = jnp.where(kpos < lim, key, INT_MIN)
        keys_ref[pl.ds(off, ATT_BLK), :] = key
        k16_ref[pl.ds(off, ATT_BLK), :] = lax.shift_right_arithmetic(key, 16).astype(I16)

    x_ref[1] = jnp.zeros(x_ref.shape[1:], F32)

    def score_two_blocks(i, carry):
        off0 = pl.multiple_of(i * SEL_BLK, SEL_BLK)
        off1 = pl.multiple_of(off0 + ATT_BLK, ATT_BLK)
        off_prev = pl.multiple_of(jnp.maximum(off0 - ATT_BLK, 0), ATT_BLK)
        score_dots(off0, 0)
        score_keys(off_prev, 1)
        score_dots(off1, 1)
        score_keys(off0, 0)
        return carry

    lax.fori_loop(0, n_sel, score_two_blocks, 0)
    score_keys(pl.multiple_of(n_sel * SEL_BLK - ATT_BLK, ATT_BLK), 1)

    @pl.when(lax.rem(n_sel, 2) == 1)
    def _():
        off = pl.multiple_of(n_sel * SEL_BLK, SEL_BLK)
        keys_ref[pl.ds(off, SEL_BLK), :] = jnp.full((SEL_BLK, QB), INT_MIN, I32)
        k16_ref[pl.ds(off, SEL_BLK), :] = jnp.full((SEL_BLK, QB), -32768, I16)

    n_cnt = lax.div(n_sel + 1, 2)

    def count_ge16(cand):
        cb = jnp.broadcast_to(cand, (PACK16, QB)).astype(I16)

        def blk(i, acc):
            off = pl.multiple_of(i * CNT_BLK, CNT_BLK)
            d = k16_ref[pl.ds(off, CNT_BLK), :].reshape(CNT_BLK // PACK16, PACK16, QB)
            hit = jnp.where(d >= cb[None], jnp.int16(1), jnp.int16(0))
            return acc + _fold(jnp.add, hit)

        acc = lax.fori_loop(0, n_cnt, blk, jnp.zeros((PACK16, QB), I16))
        return jnp.sum(acc.astype(I32), axis=0, keepdims=True)

    def bisect16(c_init):
        def body(i, st):
            t_u, c = st
            cand = t_u | lax.shift_left(jnp.int32(1), 15 - i)
            cnt = count_ge16(cand - 32768)
            ok = cnt >= topk
            return jnp.where(ok, cand, t_u), jnp.where(ok, cnt, c)

        return lax.fori_loop(0, 16, body, (jnp.zeros((1, QB), I32), c_init))

    hi_u, c_hi = bisect16(lim)
    base = (hi_u - 32768) * 65536

    def low_blk(i, carry):
        off = pl.multiple_of(i * SEL_BLK, SEL_BLK)
        key = keys_ref[pl.ds(off, SEL_BLK), :]
        v = jnp.minimum(jnp.maximum(key, base), base + 65535) - (base + 32768)
        k16_ref[pl.ds(off, SEL_BLK), :] = v.astype(I16)
        return carry

    lax.fori_loop(0, n_sel, low_blk, 0)
    lo_u, c_ge = bisect16(c_hi)
    thr = jnp.maximum(base + lo_u, INT_MIN + 1)

    def gt_blk(i, acc):
        off = pl.multiple_of(i * SEL_BLK, SEL_BLK)
        d = keys_ref[pl.ds(off, SEL_BLK), :].reshape(SEL_BLK // 8, 8, QB)
        return acc + jnp.sum(jnp.where(d > thr[None], 1, 0), axis=0)

    n_gt = jnp.sum(lax.fori_loop(0, n_sel, gt_blk, jnp.zeros((8, QB), I32)), axis=0, keepdims=True)
    need = topk - n_gt
    tie_cut = jnp.max(jnp.where((c_ge - n_gt) > need, 1, 0)) > 0

    @pl.when(jnp.logical_not(tie_cut))
    def _():
        def blk(i, carry):
            off = pl.multiple_of(i * ATT_BLK, ATT_BLK)
            key = keys_ref[pl.ds(off, ATT_BLK), :]
            bias_ref[pl.ds(off, ATT_BLK), :] = jnp.where(key >= thr, 0.0, NEG).astype(BF16)
            return carry

        lax.fori_loop(0, n_att, blk, 0)

    @pl.when(tie_cut)
    def _():
        need_f = need.astype(F32)

        def blk(i, run):
            off = pl.multiple_of(i * ATT_BLK, ATT_BLK)
            key = keys_ref[pl.ds(off, ATT_BLK), :]
            eq = key == thr
            pre = jnp.dot(tri_ref[...], jnp.where(eq, 1.0, 0.0).astype(BF16), preferred_element_type=F32)
            keep_eq = (pre + run) <= need_f
            bias = jnp.where(key > thr, 0.0, jnp.where(eq, jnp.where(keep_eq, 0.0, NEG), NEG))
            bias_ref[pl.ds(off, ATT_BLK), :] = bias.astype(BF16)
            return run + pre[ATT_BLK - 1:ATT_BLK, :]

        lax.fori_loop(0, n_att, blk, jnp.zeros((1, QB), F32))

    m_ref[...] = jnp.full(m_ref.shape, NEG, F32)
    acc_ref[...] = jnp.zeros(acc_ref.shape, F32)

    def stage_logits(off, slot):
        kblk = kk_ref[0, pl.ds(off, ATT_BLK), :]
        bias = bias_ref[pl.ds(off, ATT_BLK), :]
        bias2 = jnp.concatenate([bias, bias], axis=1)
        parts = []
        for m in range(npair):
            s = jnp.dot(kblk, qt_ref[:, 2 * m * LANES:(2 * m + 2) * LANES], preferred_element_type=F32)
            sb = s.astype(BF16) + bias2
            sb_ref[slot, m] = sb
            parts.append(_fold(jnp.maximum, sb.reshape(ATT_BLK // PACK16, PACK16, 2 * QB)))
        return parts

    def stage_probs(parts, slot):
        alphas = []
        for m in range(npair):
            m_prev = m_ref[m]
            m_new = jnp.maximum(m_prev, jnp.max(parts[m].astype(F32), axis=0, keepdims=True))
            alphas.append(jnp.exp2(m_prev - m_new))
            sb_ref[slot, m] = jnp.exp2(sb_ref[slot, m] - m_new.astype(BF16))
            m_ref[m] = m_new
        return alphas

    def stage_values(off, slot, alphas):
        for m in range(npair):
            n = (2 * m) // N_GROUP_A
            pv = jnp.dot(vt_ref[0, n * VROWS:(n + 1) * VROWS, pl.ds(off, ATT_BLK)], sb_ref[slot, m],
                         preferred_element_type=F32)
            acc_ref[m] = alphas[m] * acc_ref[m] + pv

    sb_ref[1] = jnp.full(sb_ref.shape[1:], -jnp.inf, BF16)

    def attn_two_blocks(i, parts_prev):
        off0 = pl.multiple_of(i * SEL_BLK, SEL_BLK)
        off1 = pl.multiple_of(off0 + ATT_BLK, ATT_BLK)
        off_prev = pl.multiple_of(jnp.maximum(off0 - ATT_BLK, 0), ATT_BLK)
        parts0 = stage_logits(off0, 0)
        alphas_prev = stage_probs(parts_prev, 1)
        stage_values(off_prev, 1, alphas_prev)
        alphas0 = stage_probs(parts0, 0)
        parts1 = stage_logits(off1, 1)
        stage_values(off0, 0, alphas0)
        return tuple(parts1)

    neg_parts = tuple(jnp.full((PACK16, 2 * QB), NEG, BF16) for _ in range(npair))
    parts_last = lax.fori_loop(0, n_sel, attn_two_blocks, neg_parts)
    alphas_last = stage_probs(parts_last, 1)
    stage_values(pl.multiple_of(n_sel * SEL_BLK - ATT_BLK, ATT_BLK), 1, alphas_last)

    heads = []
    for m in range(npair):
        a = acc_ref[m]
        o = a[:HD_A, :] / a[HD_A:HD_A + 1, :]
        heads += [o[:, :QB], o[:, QB:]]
    o_ref[0] = jnp.concatenate(heads, axis=0).T


def _attention(qa, qi, kiw_q, kk, kiw_k, vt, pos_off, s_real):
    b, t, _ = qa.shape
    s_pad = kk.shape[1]
    topk = min(MAX_TOPK, s_real // 4)
    idx = jnp.arange(ATT_BLK)
    tri = (idx[None, :] <= idx[:, None]).astype(BF16)
    kern = functools.partial(_attn_kernel, pos_off=pos_off, topk=topk, s_real=s_real)
    npair = N_HEADS_A // 2
    return pl.pallas_call(
        kern,
        grid=(b, t // QB),
        in_specs=[
            pl.BlockSpec((1, QB, N_HEADS_A * LANES), lambda i, j: (i, j, 0)),
            pl.BlockSpec((1, QB, N_IDX * LANES), lambda i, j: (i, j, 0)),
            pl.BlockSpec((1, QB, LANES), lambda i, j: (i, j, 0)),
            pl.BlockSpec((1, s_pad, LANES), lambda i, j: (i, 0, 0)),
            pl.BlockSpec((1, s_pad, LANES), lambda i, j: (i, 0, 0)),
            pl.BlockSpec((1, N_KV_A * VROWS, s_pad), lambda i, j: (i, 0, 0)),
            pl.BlockSpec((ATT_BLK, ATT_BLK), lambda i, j: (0, 0)),
        ],
        out_specs=pl.BlockSpec((1, QB, W_A), lambda i, j: (i, j, 0)),
        out_shape=jax.ShapeDtypeStruct((b, t, W_A), F32),
        scratch_shapes=[
            pltpu.VMEM((s_pad, QB), I32),
            pltpu.VMEM((s_pad, QB), I16),
            pltpu.VMEM((s_pad, QB), BF16),
            pltpu.VMEM((LANES, N_HEADS_A * LANES), BF16),
            pltpu.VMEM((LANES, N_IDX * LANES), BF16),
            pltpu.VMEM((npair, 1, 2 * QB), F32),
            pltpu.VMEM((npair, VROWS, 2 * QB), F32),
            pltpu.VMEM((2, npair, ATT_BLK, 2 * QB), BF16),
            pltpu.VMEM((2, ATT_BLK, N_IDX * QB), F32),
        ],
        compiler_params=pltpu.CompilerParams(dimension_semantics=("arbitrary", "arbitrary"),
                                             vmem_limit_bytes=VMEM_LIMIT),
        name="dsa_attention",
    )(qa, qi, kiw_q, kk, kiw_k, vt, tri)


def _attn_operands(ka, va, kiw, s_pad):
    b, s, _ = ka.shape
    pad = ((0, 0), (0, s_pad - s), (0, 0))
    kk = jnp.pad(ka.astype(BF16), pad)
    kiw_k = jnp.pad(kiw.astype(BF16), pad)
    vb = jnp.pad(va.astype(BF16), pad).reshape(b, s_pad, N_KV_A, HD_A)
    ones = jnp.ones((b, s_pad, N_KV_A, VROWS - HD_A), BF16)
    vaug = jnp.concatenate([vb, ones], axis=-1).reshape(b, s_pad, N_KV_A * VROWS)
    return kk, kiw_k, jnp.swapaxes(vaug, 1, 2)


def _pad_rows(a, t_pad):
    return jnp.pad(a, ((0, 0), (0, t_pad - a.shape[1]), (0, 0)))


def _ret_kernel(q_ref, k_ref, v_ref, s0_ref, dm_ref, qd_ref, kd_ref, sd_ref, o_ref, sn_ref, s_scr,
                *, rt, rc):
    t = pl.program_id(1)

    @pl.when(t == 0)
    def _():
        s_scr[...] = s0_ref[0]

    for c in range(rt // rc):
        rows = slice(c * rc, (c + 1) * rc)
        for h in range(N_HEADS_B):
            cols = slice(h * HD_B, (h + 1) * HD_B)
            q = q_ref[0, rows, cols]
            k = k_ref[0, rows, cols]
            v = v_ref[0, rows, cols]
            att = lax.dot_general(q, k, (((1,), (1,)), ((), ())), preferred_element_type=F32) * dm_ref[h]
            s = s_scr[h]
            o = (jnp.dot(att.astype(BF16), v, preferred_element_type=F32)
                 + jnp.dot(q, s.astype(BF16), preferred_element_type=F32) * qd_ref[h])
            kd = (k.astype(F32) * kd_ref[h]).astype(BF16)
            s_scr[h] = s * sd_ref[h] + lax.dot_general(kd, v, (((0,), (0,)), ((), ())),
                                                       preferred_element_type=F32)
            mu = jnp.mean(o, axis=-1, keepdims=True)
            d = o - mu
            var = jnp.mean(d * d, axis=-1, keepdims=True)
            o_ref[0, rows, cols] = d * lax.rsqrt(var + EPS)

    @pl.when(t == pl.num_programs(1) - 1)
    def _():
        sn_ref[0] = s_scr[...]


def _retention(qb_, kb_, vb_, s0):
    b, t, _ = qb_.shape
    rt = min(512, t)
    rc = min(128, t)
    lg = jnp.log1p(-jnp.exp2(-5.0 - jnp.arange(N_HEADS_B, dtype=F32)))
    i = jnp.arange(rc, dtype=F32)
    diff = i[:, None] - i[None, :]
    dmask = jnp.exp(jnp.where(diff >= 0, diff[None] * lg[:, None, None], -jnp.inf))
    ones = jnp.ones((1, 1, HD_B), F32)
    q_dec = jnp.exp((i + 1.0)[None, :] * lg[:, None])[:, :, None] * ones
    k_dec = jnp.exp((rc - 1.0 - i)[None, :] * lg[:, None])[:, :, None] * ones
    s_dec = jnp.exp(rc * lg)[:, None, None] * jnp.ones((1, HD_B, HD_B), F32)
    blk = pl.BlockSpec((1, rt, W_B), lambda i_, j: (i_, j, 0))
    st = pl.BlockSpec((1, N_HEADS_B, HD_B, HD_B), lambda i_, j: (i_, 0, 0, 0))
    c3 = lambda i_, j: (0, 0, 0)
    kern = functools.partial(_ret_kernel, rt=rt, rc=rc)
    return pl.pallas_call(
        kern,
        grid=(b, t // rt),
        in_specs=[blk, blk, blk, st,
                  pl.BlockSpec((N_HEADS_B, rc, rc), c3),
                  pl.BlockSpec((N_HEADS_B, rc, HD_B), c3),
                  pl.BlockSpec((N_HEADS_B, rc, HD_B), c3),
                  pl.BlockSpec((N_HEADS_B, HD_B, HD_B), c3)],
        out_specs=[blk, st],
        out_shape=[jax.ShapeDtypeStruct((b, t, W_B), F32),
                   jax.ShapeDtypeStruct((b, N_HEADS_B, HD_B, HD_B), F32)],
        scratch_shapes=[pltpu.VMEM((N_HEADS_B, HD_B, HD_B), F32)],
        compiler_params=pltpu.CompilerParams(dimension_semantics=("arbitrary", "arbitrary"),
                                             vmem_limit_bytes=VMEM_LIMIT),
        name="retention",
    )(qb_, kb_, vb_, s0, dmask, q_dec, k_dec, s_dec)


def _outproj_kernel(x_ref, oa_ref, ga_ref, ob_ref, gb_ref, w_ref, g_ref, gate_ref, o_ref):
    ma = (oa_ref[0] * ga_ref[0].astype(F32)).astype(BF16)
    mb = (ob_ref[0] * gb_ref[0].astype(F32)).astype(BF16)
    y = (jnp.dot(ma, w_ref[:W_A, :], preferred_element_type=F32)
         + jnp.dot(mb, w_ref[W_A:, :], preferred_element_type=F32))
    ms = jnp.mean(y * y, axis=-1, keepdims=True)
    yn = y * lax.rsqrt(ms + EPS) * g_ref[...]
    o_ref[0] = x_ref[0] + gate_ref[0] * yn


def _outproj(x, oa, ga, ob, gb, w_out, g_post, gate):
    b, t, _ = x.shape
    tm = _row_tile(t)
    row = lambda w: pl.BlockSpec((1, tm, w), lambda i, j: (i, j, 0))
    return pl.pallas_call(
        _outproj_kernel,
        grid=(b, t // tm),
        in_specs=[row(D_MODEL), row(W_A), row(W_A), row(W_B), row(W_B),
                  pl.BlockSpec((D_MODEL, D_MODEL), lambda i, j: (0, 0)),
                  pl.BlockSpec((1, D_MODEL), lambda i, j: (0, 0)),
                  pl.BlockSpec((1, 1, D_MODEL), lambda i, j: (i, 0, 0))],
        out_specs=row(D_MODEL),
        out_shape=jax.ShapeDtypeStruct((b, t, D_MODEL), F32),
        compiler_params=pltpu.CompilerParams(dimension_semantics=("arbitrary", "arbitrary"),
                                             vmem_limit_bytes=VMEM_LIMIT),
        name="outproj",
    )(x, oa, ga, ob, gb, w_out.astype(BF16), g_post.reshape(1, D_MODEL), gate)


def _layer(x, mod, pos_off, tabs, wp, wr, g_pre, w_out, g_post, past):
    b, t, _ = x.shape
    shift, scale, gate = jnp.split(mod, 3, axis=-1)
    p = _inproj(x, (1.0 + scale)[:, None, :], shift[:, None, :], g_pre, wp, wr, tabs)
    ka, va, kiw = p["ka"], p["va"], p["kiw"]
    if past is None:
        k_all, v_all, kiw_all = ka, va, kiw
        s0 = jnp.zeros((b, N_HEADS_B, HD_B, HD_B), F32)
    else:
        pk, pv, pki, s0 = past
        plen = pk.shape[1]
        k_all = jnp.concatenate([pk.reshape(b, plen, N_KV_A * HD_A), ka], axis=1)
        v_all = jnp.concatenate([pv.reshape(b, plen, N_KV_A * HD_A), va], axis=1)
        pki = jnp.pad(pki, ((0, 0), (0, 0), (0, LANES - IDX_DIM)))
        kiw_all = jnp.concatenate([pki, kiw], axis=1)
    s_real = k_all.shape[1]
    s_pad = -(-s_real // CNT_BLK) * CNT_BLK
    t_pad = -(-t // QB) * QB
    kk, kiw_k, vt = _attn_operands(k_all, v_all, kiw_all, s_pad)
    oa = _attention(_pad_rows(p["qa"], t_pad), _pad_rows(p["qi"], t_pad), _pad_rows(kiw, t_pad),
                    kk, kiw_k, vt, pos_off, s_real)[:, :t]
    ob, s_new = _retention(p["qb"], p["kb"], p["vb"], s0)
    x_new = _outproj(x, oa, p["ga"], ob, p["gb"], w_out, g_post, gate[:, None, :])
    k_n = ka.reshape(b, t, N_KV_A, HD_A)
    v_n = va.reshape(b, t, N_KV_A, HD_A)
    return x_new, (k_n, v_n, kiw[..., :IDX_DIM], s_new)


def kernel(x_prompt, x_sample, cache_k_a, cache_v_a, cache_kidx_a, state_s_b, c_prompt, c_sample,
           w_ada, b_ada, g_pre, w_in, w_out, g_post):
    depth = w_in.shape[0]
    nb = x_prompt.shape[0]
    past_len = cache_k_a.shape[2]
    tabs_p = _rope_tables(jnp.arange(x_prompt.shape[1], dtype=jnp.int32))
    tabs_s = _rope_tables(past_len + jnp.arange(x_sample.shape[1], dtype=jnp.int32))
    mod = _modulation(jnp.concatenate([c_prompt, c_sample], axis=0), w_ada, b_ada)
    hp, hs = x_prompt, x_sample
    outs_p, outs_s = [], []
    for l in range(depth):
        wp, wr = _prep_w_in(w_in[l])
        hp, new_p = _layer(hp, mod[l, :nb], 0, tabs_p, wp, wr, g_pre[l], w_out[l], g_post[l], None)
        outs_p.append(new_p)
        hs, new_s = _layer(hs, mod[l, nb:], past_len, tabs_s, wp, wr, g_pre[l], w_out[l], g_post[l],
                           (cache_k_a[l], cache_v_a[l], cache_kidx_a[l], state_s_b[l]))
        outs_s.append(new_s)
    stack = lambda outs, i: jnp.stack([o[i] for o in outs])
    return (hp, hs,
            stack(outs_p, 0), stack(outs_p, 1), stack(outs_p, 2), stack(outs_p, 3),
            stack(outs_s, 0), stack(outs_s, 1), stack(outs_s, 2), stack(outs_s, 3))
```

```python
import functools
import math

import jax
import jax.numpy as jnp
from jax import lax
from jax.experimental import pallas as pl
from jax.experimental.pallas import tpu as pltpu

D_MODEL = 1024
CHUNK = 64
W_A = 512
HD_A = 64
N_HEADS_A = 8
N_KV_A = 2
N_GROUP_A = N_HEADS_A // N_KV_A
N_IDX = 4
IDX_DIM = 64
MAX_TOPK = 256
HD_B = 128
N_HEADS_B = 4
W_B = 512
ROPE_THETA = 10000.0
EPS = 1e-6
IDX_SCALE = (IDX_DIM ** -0.5) * (N_IDX ** -0.5)
LOG2E = math.log2(math.e)

LANES = 128
PACK16 = 16
VMEM_LIMIT = 56 * 1024 * 1024
INT_MIN = -(2 ** 31)
NEG = -1e30

QB = LANES
SEL_BLK = 512
ATT_BLK = 256
VROWS = HD_A + PACK16

F32 = jnp.float32
BF16 = jnp.bfloat16
I16 = jnp.int16
I32 = jnp.int32

_SEGS = (
    ("qa", N_HEADS_A * LANES, "a"),
    ("ka", N_KV_A * HD_A, "a"),
    ("va", N_KV_A * HD_A, None),
    ("ga", W_A, None),
    ("qi", N_IDX * LANES, "a"),
    ("kiw", LANES, "k"),
    ("qb", W_B, "b"),
    ("kb", W_B, "b"),
    ("vb", W_B, None),
    ("gb", W_B, None),
)
_W_COLS = sum(s[1] for s in _SEGS)
_R_COLS = sum(s[1] for s in _SEGS if s[2] is not None)
_OUT_DTYPES = {"qa": BF16, "ka": F32, "va": F32, "ga": BF16, "qi": BF16, "kiw": F32,
               "qb": BF16, "kb": BF16, "vb": BF16, "gb": BF16}


def _silu(x):
    return x / (1.0 + jnp.exp(-x))


def _fold(op, x):
    parts = [x[i] for i in range(x.shape[0])]
    while len(parts) > 1:
        parts = [op(parts[i], parts[i + 1]) for i in range(0, len(parts) - 1, 2)] + parts[len(parts) & ~1:]
    return parts[0]


def _mod_kernel(c_ref, w_ref, b_ref, o_ref):
    c = c_ref[...]
    s = _silu(c)
    o_ref[0] = jnp.dot(s.astype(BF16), w_ref[0].astype(BF16), preferred_element_type=F32) + b_ref[0]


def _modulation(c_all, w_ada, b_ada):
    depth = w_ada.shape[0]
    n = c_all.shape[0]
    nblk = 3
    return pl.pallas_call(
        _mod_kernel,
        grid=(depth, nblk),
        in_specs=[
            pl.BlockSpec((n, D_MODEL), lambda l, j: (0, 0)),
            pl.BlockSpec((1, D_MODEL, D_MODEL), lambda l, j: (l, 0, j)),
            pl.BlockSpec((1, 1, D_MODEL), lambda l, j: (l, 0, j)),
        ],
        out_specs=pl.BlockSpec((1, n, D_MODEL), lambda l, j: (l, 0, j)),
        out_shape=jax.ShapeDtypeStruct((depth, n, 3 * D_MODEL), F32),
        compiler_params=pltpu.CompilerParams(dimension_semantics=("arbitrary", "arbitrary"),
                                             vmem_limit_bytes=VMEM_LIMIT),
        name="modulation",
    )(c_all, w_ada, b_ada.reshape(depth, 1, 3 * D_MODEL))


def _inproj_kernel(x_ref, sc_ref, sh_ref, g_ref, w_ref, wr_ref,
                   ca_ref, sa_ref, ck_ref, sk_ref, cb_ref, sb_ref, *out_refs):
    x = x_ref[0]
    ms = jnp.mean(x * x, axis=-1, keepdims=True)
    y = x * lax.rsqrt(ms + EPS) * g_ref[...]
    h = y * sc_ref[0] + sh_ref[0]
    hb = h.astype(BF16)
    tabs = {"a": (ca_ref, sa_ref), "k": (ck_ref, sk_ref), "b": (cb_ref, sb_ref)}
    c0 = 0
    r0 = 0
    for (name, width, kind), o_ref in zip(_SEGS, out_refs):
        step = 2 * LANES if width % (2 * LANES) == 0 else LANES
        for off in range(0, width, step):
            p = jnp.dot(hb, w_ref[:, c0 + off:c0 + off + step], preferred_element_type=F32)
            if kind is not None:
                r = jnp.dot(hb, wr_ref[:, r0 + off:r0 + off + step], preferred_element_type=F32)
                cos = jnp.tile(tabs[kind][0][...], (1, step // LANES))
                sin = jnp.tile(tabs[kind][1][...], (1, step // LANES))
                p = p * cos + r * sin
            if name == "kb":
                p = p * (HD_B ** -0.5)
            if name in ("ga", "gb"):
                p = _silu(p)
            o_ref[0, :, off:off + step] = p.astype(o_ref.dtype)
        c0 += width
        if kind is not None:
            r0 += width


def _rot_cols(w, d):
    k = w.shape[0]
    w3 = w.reshape(k, -1, d)
    return jnp.concatenate([-w3[..., d // 2:], w3[..., :d // 2]], axis=-1).reshape(k, -1)


def _widen_heads(w, nkv_of_head):
    k = w.shape[0]
    nh = w.shape[1] // HD_A
    z = jnp.zeros((k, HD_A), w.dtype)
    cols = []
    for h in range(nh):
        wh = w[:, h * HD_A:(h + 1) * HD_A]
        cols += [wh, z] if nkv_of_head(h) == 0 else [z, wh]
    return jnp.concatenate(cols, axis=1)


def _prep_w_in(w):
    sizes = (W_A, N_KV_A * HD_A, N_KV_A * HD_A, W_A, N_IDX * IDX_DIM, IDX_DIM, N_IDX, W_B, W_B, W_B, W_B)
    parts = []
    o = 0
    for n in sizes:
        parts.append(w[:, o:o + n])
        o += n
    q_a, k_a, v_a, g_a, q_i, k_i, w_i, q_b, k_b, v_b, g_b = parts
    kk = w.shape[0]
    q_a = q_a * ((HD_A ** -0.5) * LOG2E)
    kiw = jnp.concatenate([k_i, w_i, jnp.zeros((kk, LANES - IDX_DIM - N_IDX), w.dtype)], axis=1)
    kiw_rot = jnp.concatenate([_rot_cols(k_i, IDX_DIM), jnp.zeros((kk, LANES - IDX_DIM), w.dtype)], axis=1)
    qa_head = lambda h: h // N_GROUP_A
    qi_head = lambda h: 0
    segs = {
        "qa": (_widen_heads(q_a, qa_head), _widen_heads(_rot_cols(q_a, HD_A), qa_head)),
        "ka": (k_a, _rot_cols(k_a, HD_A)),
        "va": (v_a, None),
        "ga": (g_a, None),
        "qi": (_widen_heads(q_i, qi_head), _widen_heads(_rot_cols(q_i, IDX_DIM), qi_head)),
        "kiw": (kiw, kiw_rot),
        "qb": (q_b, _rot_cols(q_b, HD_B)),
        "kb": (k_b, _rot_cols(k_b, HD_B)),
        "vb": (v_b, None),
        "gb": (g_b, None),
    }
    wp = jnp.concatenate([segs[s[0]][0] for s in _SEGS], axis=1).astype(BF16)
    wr = jnp.concatenate([segs[s[0]][1] for s in _SEGS if s[2] is not None], axis=1).astype(BF16)
    return wp, wr


def _rope_tables(pos):
    posf = pos.astype(F32)[:, None]

    def tab(d):
        half = d // 2
        inv = ROPE_THETA ** (-jnp.arange(half, dtype=F32) * 2.0 / d)
        ang = posf * inv[None, :]
        reps = LANES // half
        return jnp.tile(jnp.cos(ang), (1, reps)), jnp.tile(jnp.sin(ang), (1, reps))

    ca, sa = tab(HD_A)
    cb, sb = tab(HD_B)
    lane = jnp.arange(LANES)[None, :]
    ck = jnp.where(lane < IDX_DIM, ca, 1.0)
    sk = jnp.where(lane < IDX_DIM, sa, 0.0)
    return ca, sa, ck, sk, cb, sb


def _row_tile(t):
    return min(512, t)


def _inproj(x, scale1p, shift, g_pre, wp, wr, tabs):
    b, t, _ = x.shape
    tm = _row_tile(t)
    nt = t // tm
    const = lambda i, j: (0, 0)
    tab_spec = pl.BlockSpec((tm, LANES), lambda i, j: (j, 0))
    in_specs = [
        pl.BlockSpec((1, tm, D_MODEL), lambda i, j: (i, j, 0)),
        pl.BlockSpec((1, 1, D_MODEL), lambda i, j: (i, 0, 0)),
        pl.BlockSpec((1, 1, D_MODEL), lambda i, j: (i, 0, 0)),
        pl.BlockSpec((1, D_MODEL), const),
        pl.BlockSpec((D_MODEL, _W_COLS), const, pipeline_mode=pl.Buffered(1)),
        pl.BlockSpec((D_MODEL, _R_COLS), const, pipeline_mode=pl.Buffered(1)),
    ] + [tab_spec] * 6
    out_specs = [pl.BlockSpec((1, tm, s[1]), lambda i, j: (i, j, 0)) for s in _SEGS]
    out_shape = [jax.ShapeDtypeStruct((b, t, s[1]), _OUT_DTYPES[s[0]]) for s in _SEGS]
    outs = pl.pallas_call(
        _inproj_kernel,
        grid=(b, nt),
        in_specs=in_specs,
        out_specs=out_specs,
        out_shape=out_shape,
        compiler_params=pltpu.CompilerParams(dimension_semantics=("arbitrary", "arbitrary"),
                                             vmem_limit_bytes=VMEM_LIMIT),
        name="inproj",
    )(x, scale1p, shift, g_pre.reshape(1, D_MODEL), wp, wr, *tabs)
    return dict(zip([s[0] for s in _SEGS], outs))


def _attn_kernel(qa_ref, qi_ref, wq_ref, kk_ref, kiw_ref, vt_ref, tri_ref, o_ref,
                 keys_ref, planes_ref, alive_ref, bias_ref, qt_ref, qit_ref, m_ref, acc_ref, sb_ref, x_ref,
                 *, pos_off, topk, s_real):
    j = pl.program_id(1)
    npair = N_HEADS_A // 2

    qlane = lax.broadcasted_iota(I32, (1, QB), 1)
    pos = pos_off + j * QB + qlane
    lim = jnp.minimum((lax.shift_right_logical(pos, 6) + 1) * CHUNK, s_real)
    last_pos = pos_off + (j + 1) * QB - 1
    lim_max = jnp.minimum((lax.shift_right_logical(last_pos, 6) + 1) * CHUNK, s_real)
    n_sel = lax.div(lim_max + (SEL_BLK - 1), SEL_BLK)
    n_att = n_sel * (SEL_BLK // ATT_BLK)

    for hd in range(N_HEADS_A):
        sl = slice(hd * LANES, (hd + 1) * LANES)
        qt_ref[:, sl] = qa_ref[0, :, sl].astype(F32).T.astype(BF16)
    for h in range(N_IDX):
        sl = slice(h * LANES, (h + 1) * LANES)
        qit_ref[:, sl] = qi_ref[0, :, sl].astype(F32).T.astype(BF16)
    wt = wq_ref[0].T
    w_idx = [wt[IDX_DIM + h:IDX_DIM + h + 1, :] * IDX_SCALE for h in range(N_IDX)]

    def score_dots(off, slot):
        kblk = kiw_ref[0, pl.ds(off, ATT_BLK), :]
        for m in range(N_IDX // 2):
            cols = slice(2 * m * LANES, (2 * m + 2) * LANES)
            x_ref[slot, :, cols] = jnp.dot(kblk, qit_ref[:, cols], preferred_element_type=F32)

    def score_keys(off, slot):
        sc = None
        for h in range(N_IDX):
            term = jnp.maximum(x_ref[slot, :, h * LANES:(h + 1) * LANES], 0.0) * w_idx[h]
            sc = term if sc is None else sc + term
        bits = pltpu.bitcast(sc, I32)
        key = bits ^ (lax.shift_right_arithmetic(bits, 31) & 0x7FFFFFFF)
        kpos = off + lax.broadcasted_iota(I32, (ATT_BLK, QB), 0)
        key = jnp.where(kpos < lim, key, INT_MIN)
        keys_ref[pl.ds(off, ATT_BLK), :] = key

    x_ref[1] = jnp.zeros(x_ref.shape[1:], F32)

    def score_two_blocks(i, carry):
        off0 = pl.multiple_of(i * SEL_BLK, SEL_BLK)
        off1 = pl.multiple_of(off0 + ATT_BLK, ATT_BLK)
        off_prev = pl.multiple_of(jnp.maximum(off0 - ATT_BLK, 0), ATT_BLK)
        score_dots(off0, 0)
        score_keys(off_prev, 1)
        score_dots(off1, 1)
        score_keys(off0, 0)
        return carry

    lax.fori_loop(0, n_sel, score_two_blocks, 0)
    score_keys(pl.multiple_of(n_sel * SEL_BLK - ATT_BLK, ATT_BLK), 1)

    n_blk = n_sel * (SEL_BLK // ATT_BLK)
    vregs = ATT_BLK // 8

    def plane_blk(i, carry):
        off = pl.multiple_of(i * ATT_BLK, ATT_BLK)
        blk = keys_ref[pl.ds(off, ATT_BLK), :]
        a = [blk[8 * r:8 * r + 8, :] ^ INT_MIN for r in range(vregs)]
        for j, msk in ((16, 0x0000FFFF), (8, 0x00FF00FF), (4, 0x0F0F0F0F), (2, 0x33333333), (1, 0x55555555)):
            for k in range(vregs):
                if k & j == 0:
                    t = (a[k] ^ lax.shift_right_logical(a[k + j], j)) & msk
                    a[k] = a[k] ^ t
                    a[k + j] = a[k + j] ^ lax.shift_left(t, j)
        for r in range(vregs):
            planes_ref[i, r] = a[r]
        return carry

    lax.fori_loop(0, n_blk, plane_blk, 0)
    alive_ref[...] = jnp.full(alive_ref.shape, -1, I32)

    def count_ones(p_cur, p_prev, flip):
        def blk(i, acc):
            for i_blk in (2 * i, 2 * i + 1):
                alive = alive_ref[i_blk]
                if p_prev is not None:
                    alive = alive & (planes_ref[i_blk, p_prev] ^ flip)
                    alive_ref[i_blk] = alive
                acc = acc + lax.population_count(alive & planes_ref[i_blk, p_cur])
            return acc

        acc = lax.fori_loop(0, n_sel, blk, jnp.zeros((8, QB), I32))
        return jnp.sum(acc, axis=0, keepdims=True)

    def decide(p, ones, st):
        thr_u, rem, _ = st
        take = ones >= rem
        bit = lax.shift_left(jnp.int32(1), 31 - p)
        return (jnp.where(take, thr_u | bit, thr_u), jnp.where(take, rem, rem - ones),
                jnp.where(take, 0, -1))

    zero = jnp.zeros((1, QB), I32)
    st = decide(0, count_ones(0, None, None), (zero, zero + topk, zero))
    st = lax.fori_loop(1, 32, lambda p, s: decide(p, count_ones(p, p - 1, s[2]), s), st)
    thr_u, need, flip = st

    def eq_blk(i, acc):
        return acc + lax.population_count(alive_ref[i] & (planes_ref[i, 31] ^ flip))

    n_eq = jnp.sum(lax.fori_loop(0, n_blk, eq_blk, jnp.zeros((8, QB), I32)), axis=0, keepdims=True)
    thr = jnp.maximum(thr_u ^ INT_MIN, INT_MIN + 1)
    tie_cut = jnp.max(jnp.where((n_eq > need) & (thr_u != 0), 1, 0)) > 0

    @pl.when(jnp.logical_not(tie_cut))
    def _():
        def blk(i, carry):
            off = pl.multiple_of(i * ATT_BLK, ATT_BLK)
            key = keys_ref[pl.ds(off, ATT_BLK), :]
            bias_ref[pl.ds(off, ATT_BLK), :] = jnp.where(key >= thr, 0.0, NEG).astype(BF16)
            return carry

        lax.fori_loop(0, n_att, blk, 0)

    @pl.when(tie_cut)
    def _():
        need_f = need.astype(F32)

        def blk(i, run):
            off = pl.multiple_of(i * ATT_BLK, ATT_BLK)
            key = keys_ref[pl.ds(off, ATT_BLK), :]
            eq = key == thr
            pre = jnp.dot(tri_ref[...], jnp.where(eq, 1.0, 0.0).astype(BF16), preferred_element_type=F32)
            keep_eq = (pre + run) <= need_f
            bias = jnp.where(key > thr, 0.0, jnp.where(eq, jnp.where(keep_eq, 0.0, NEG), NEG))
            bias_ref[pl.ds(off, ATT_BLK), :] = bias.astype(BF16)
            return run + pre[ATT_BLK - 1:ATT_BLK, :]

        lax.fori_loop(0, n_att, blk, jnp.zeros((1, QB), F32))

    m_ref[...] = jnp.full(m_ref.shape, NEG, F32)
    acc_ref[...] = jnp.zeros(acc_ref.shape, F32)

    def stage_logits(off, slot):
        kblk = kk_ref[0, pl.ds(off, ATT_BLK), :]
        bias = bias_ref[pl.ds(off, ATT_BLK), :]
        bias2 = jnp.concatenate([bias, bias], axis=1)
        parts = []
        for m in range(npair):
            s = jnp.dot(kblk, qt_ref[:, 2 * m * LANES:(2 * m + 2) * LANES], preferred_element_type=F32)
            sb = s.astype(BF16) + bias2
            sb_ref[slot, m] = sb
            parts.append(_fold(jnp.maximum, sb.reshape(ATT_BLK // PACK16, PACK16, 2 * QB)))
        return parts

    def stage_probs(parts, slot):
        alphas = []
        for m in range(npair):
            m_prev = m_ref[m]
            m_new = jnp.maximum(m_prev, jnp.max(parts[m].astype(F32), axis=0, keepdims=True))
            alphas.append(jnp.exp2(m_prev - m_new))
            sb_ref[slot, m] = jnp.exp2(sb_ref[slot, m] - m_new.astype(BF16))
            m_ref[m] = m_new
        return alphas

    def stage_values(off, slot, alphas):
        for m in range(npair):
            n = (2 * m) // N_GROUP_A
            pv = jnp.dot(vt_ref[0, n * VROWS:(n + 1) * VROWS, pl.ds(off, ATT_BLK)], sb_ref[slot, m],
                         preferred_element_type=F32)
            acc_ref[m] = alphas[m] * acc_ref[m] + pv

    sb_ref[1] = jnp.full(sb_ref.shape[1:], -jnp.inf, BF16)

    def attn_two_blocks(i, parts_prev):
        off0 = pl.multiple_of(i * SEL_BLK, SEL_BLK)
        off1 = pl.multiple_of(off0 + ATT_BLK, ATT_BLK)
        off_prev = pl.multiple_of(jnp.maximum(off0 - ATT_BLK, 0), ATT_BLK)
        parts0 = stage_logits(off0, 0)
        alphas_prev = stage_probs(parts_prev, 1)
        stage_values(off_prev, 1, alphas_prev)
        alphas0 = stage_probs(parts0, 0)
        parts1 = stage_logits(off1, 1)
        stage_values(off0, 0, alphas0)
        return tuple(parts1)

    neg_parts = tuple(jnp.full((PACK16, 2 * QB), NEG, BF16) for _ in range(npair))
    parts_last = lax.fori_loop(0, n_sel, attn_two_blocks, neg_parts)
    alphas_last = stage_probs(parts_last, 1)
    stage_values(pl.multiple_of(n_sel * SEL_BLK - ATT_BLK, ATT_BLK), 1, alphas_last)

    heads = []
    for m in range(npair):
        a = acc_ref[m]
        o = a[:HD_A, :] / a[HD_A:HD_A + 1, :]
        heads += [o[:, :QB], o[:, QB:]]
    o_ref[0] = jnp.concatenate(heads, axis=0).T


def _attention(qa, qi, kiw_q, kk, kiw_k, vt, pos_off, s_real):
    b, t, _ = qa.shape
    s_pad = kk.shape[1]
    topk = min(MAX_TOPK, s_real // 4)
    idx = jnp.arange(ATT_BLK)
    tri = (idx[None, :] <= idx[:, None]).astype(BF16)
    kern = functools.partial(_attn_kernel, pos_off=pos_off, topk=topk, s_real=s_real)
    npair = N_HEADS_A // 2
    return pl.pallas_call(
        kern,
        grid=(b, t // QB),
        in_specs=[
            pl.BlockSpec((1, QB, N_HEADS_A * LANES), lambda i, j: (i, j, 0)),
            pl.BlockSpec((1, QB, N_IDX * LANES), lambda i, j: (i, j, 0)),
            pl.BlockSpec((1, QB, LANES), lambda i, j: (i, j, 0)),
            pl.BlockSpec((1, s_pad, LANES), lambda i, j: (i, 0, 0)),
            pl.BlockSpec((1, s_pad, LANES), lambda i, j: (i, 0, 0)),
            pl.BlockSpec((1, N_KV_A * VROWS, s_pad), lambda i, j: (i, 0, 0)),
            pl.BlockSpec((ATT_BLK, ATT_BLK), lambda i, j: (0, 0)),
        ],
        out_specs=pl.BlockSpec((1, QB, W_A), lambda i, j: (i, j, 0)),
        out_shape=jax.ShapeDtypeStruct((b, t, W_A), F32),
        scratch_shapes=[
            pltpu.VMEM((s_pad, QB), I32),
            pltpu.VMEM((s_pad // ATT_BLK, ATT_BLK // 8, 8, QB), I32),
            pltpu.VMEM((s_pad // ATT_BLK, 8, QB), I32),
            pltpu.VMEM((s_pad, QB), BF16),
            pltpu.VMEM((LANES, N_HEADS_A * LANES), BF16),
            pltpu.VMEM((LANES, N_IDX * LANES), BF16),
            pltpu.VMEM((npair, 1, 2 * QB), F32),
            pltpu.VMEM((npair, VROWS, 2 * QB), F32),
            pltpu.VMEM((2, npair, ATT_BLK, 2 * QB), BF16),
            pltpu.VMEM((2, ATT_BLK, N_IDX * QB), F32),
        ],
        compiler_params=pltpu.CompilerParams(dimension_semantics=("arbitrary", "arbitrary"),
                                             vmem_limit_bytes=VMEM_LIMIT),
        name="dsa_attention",
    )(qa, qi, kiw_q, kk, kiw_k, vt, tri)


def _attn_operands(ka, va, kiw, s_pad):
    b, s, _ = ka.shape
    pad = ((0, 0), (0, s_pad - s), (0, 0))
    kk = jnp.pad(ka.astype(BF16), pad)
    kiw_k = jnp.pad(kiw.astype(BF16), pad)
    vb = jnp.pad(va.astype(BF16), pad).reshape(b, s_pad, N_KV_A, HD_A)
    ones = jnp.ones((b, s_pad, N_KV_A, VROWS - HD_A), BF16)
    vaug = jnp.concatenate([vb, ones], axis=-1).reshape(b, s_pad, N_KV_A * VROWS)
    return kk, kiw_k, jnp.swapaxes(vaug, 1, 2)


def _pad_rows(a, t_pad):
    return jnp.pad(a, ((0, 0), (0, t_pad - a.shape[1]), (0, 0)))


def _ret_kernel(q_ref, k_ref, v_ref, s0_ref, dm_ref, qd_ref, kd_ref, sd_ref, o_ref, sn_ref, s_scr,
                *, rt, rc):
    t = pl.program_id(1)

    @pl.when(t == 0)
    def _():
        s_scr[...] = s0_ref[0]

    for c in range(rt // rc):
        rows = slice(c * rc, (c + 1) * rc)
        for h in range(N_HEADS_B):
            cols = slice(h * HD_B, (h + 1) * HD_B)
            q = q_ref[0, rows, cols]
            k = k_ref[0, rows, cols]
            v = v_ref[0, rows, cols]
            att = lax.dot_general(q, k, (((1,), (1,)), ((), ())), preferred_element_type=F32) * dm_ref[h]
            s = s_scr[h]
            o = (jnp.dot(att.astype(BF16), v, preferred_element_type=F32)
                 + jnp.dot(q, s.astype(BF16), preferred_element_type=F32) * qd_ref[h])
            kd = (k.astype(F32) * kd_ref[h]).astype(BF16)
            s_scr[h] = s * sd_ref[h] + lax.dot_general(kd, v, (((0,), (0,)), ((), ())),
                                                       preferred_element_type=F32)
            mu = jnp.mean(o, axis=-1, keepdims=True)
            d = o - mu
            var = jnp.mean(d * d, axis=-1, keepdims=True)
            o_ref[0, rows, cols] = d * lax.rsqrt(var + EPS)

    @pl.when(t == pl.num_programs(1) - 1)
    def _():
        sn_ref[0] = s_scr[...]


def _retention(qb_, kb_, vb_, s0):
    b, t, _ = qb_.shape
    rt = min(512, t)
    rc = min(128, t)
    lg = jnp.log1p(-jnp.exp2(-5.0 - jnp.arange(N_HEADS_B, dtype=F32)))
    i = jnp.arange(rc, dtype=F32)
    diff = i[:, None] - i[None, :]
    dmask = jnp.exp(jnp.where(diff >= 0, diff[None] * lg[:, None, None], -jnp.inf))
    ones = jnp.ones((1, 1, HD_B), F32)
    q_dec = jnp.exp((i + 1.0)[None, :] * lg[:, None])[:, :, None] * ones
    k_dec = jnp.exp((rc - 1.0 - i)[None, :] * lg[:, None])[:, :, None] * ones
    s_dec = jnp.exp(rc * lg)[:, None, None] * jnp.ones((1, HD_B, HD_B), F32)
    blk = pl.BlockSpec((1, rt, W_B), lambda i_, j: (i_, j, 0))
    st = pl.BlockSpec((1, N_HEADS_B, HD_B, HD_B), lambda i_, j: (i_, 0, 0, 0))
    c3 = lambda i_, j: (0, 0, 0)
    kern = functools.partial(_ret_kernel, rt=rt, rc=rc)
    return pl.pallas_call(
        kern,
        grid=(b, t // rt),
        in_specs=[blk, blk, blk, st,
                  pl.BlockSpec((N_HEADS_B, rc, rc), c3),
                  pl.BlockSpec((N_HEADS_B, rc, HD_B), c3),
                  pl.BlockSpec((N_HEADS_B, rc, HD_B), c3),
                  pl.BlockSpec((N_HEADS_B, HD_B, HD_B), c3)],
        out_specs=[blk, st],
        out_shape=[jax.ShapeDtypeStruct((b, t, W_B), F32),
                   jax.ShapeDtypeStruct((b, N_HEADS_B, HD_B, HD_B), F32)],
        scratch_shapes=[pltpu.VMEM((N_HEADS_B, HD_B, HD_B), F32)],
        compiler_params=pltpu.CompilerParams(dimension_semantics=("arbitrary", "arbitrary"),
                                             vmem_limit_bytes=VMEM_LIMIT),
        name="retention",
    )(qb_, kb_, vb_, s0, dmask, q_dec, k_dec, s_dec)


def _outproj_kernel(x_ref, oa_ref, ga_ref, ob_ref, gb_ref, w_ref, g_ref, gate_ref, o_ref):
    ma = (oa_ref[0] * ga_ref[0].astype(F32)).astype(BF16)
    mb = (ob_ref[0] * gb_ref[0].astype(F32)).astype(BF16)
    y = (jnp.dot(ma, w_ref[:W_A, :], preferred_element_type=F32)
         + jnp.dot(mb, w_ref[W_A:, :], preferred_element_type=F32))
    ms = jnp.mean(y * y, axis=-1, keepdims=True)
    yn = y * lax.rsqrt(ms + EPS) * g_ref[...]
    o_ref[0] = x_ref[0] + gate_ref[0] * yn


def _outproj(x, oa, ga, ob, gb, w_out, g_post, gate):
    b, t, _ = x.shape
    tm = _row_tile(t)
    row = lambda w: pl.BlockSpec((1, tm, w), lambda i, j: (i, j, 0))
    return pl.pallas_call(
        _outproj_kernel,
        grid=(b, t // tm),
        in_specs=[row(D_MODEL), row(W_A), row(W_A), row(W_B), row(W_B),
                  pl.BlockSpec((D_MODEL, D_MODEL), lambda i, j: (0, 0)),
                  pl.BlockSpec((1, D_MODEL), lambda i, j: (0, 0)),
                  pl.BlockSpec((1, 1, D_MODEL), lambda i, j: (i, 0, 0))],
        out_specs=row(D_MODEL),
        out_shape=jax.ShapeDtypeStruct((b, t, D_MODEL), F32),
        compiler_params=pltpu.CompilerParams(dimension_semantics=("arbitrary", "arbitrary"),
                                             vmem_limit_bytes=VMEM_LIMIT),
        name="outproj",
    )(x, oa, ga, ob, gb, w_out.astype(BF16), g_post.reshape(1, D_MODEL), gate)


def _layer(x, mod, pos_off, tabs, wp, wr, g_pre, w_out, g_post, past):
    b, t, _ = x.shape
    shift, scale, gate = jnp.split(mod, 3, axis=-1)
    p = _inproj(x, (1.0 + scale)[:, None, :], shift[:, None, :], g_pre, wp, wr, tabs)
    ka, va, kiw = p["ka"], p["va"], p["kiw"]
    if past is None:
        k_all, v_all, kiw_all = ka, va, kiw
        s0 = jnp.zeros((b, N_HEADS_B, HD_B, HD_B), F32)
    else:
        pk, pv, pki, s0 = past
        plen = pk.shape[1]
        k_all = jnp.concatenate([pk.reshape(b, plen, N_KV_A * HD_A), ka], axis=1)
        v_all = jnp.concatenate([pv.reshape(b, plen, N_KV_A * HD_A), va], axis=1)
        pki = jnp.pad(pki, ((0, 0), (0, 0), (0, LANES - IDX_DIM)))
        kiw_all = jnp.concatenate([pki, kiw], axis=1)
    s_real = k_all.shape[1]
    s_pad = -(-s_real // SEL_BLK) * SEL_BLK
    t_pad = -(-t // QB) * QB
    kk, kiw_k, vt = _attn_operands(k_all, v_all, kiw_all, s_pad)
    oa = _attention(_pad_rows(p["qa"], t_pad), _pad_rows(p["qi"], t_pad), _pad_rows(kiw, t_pad),
                    kk, kiw_k, vt, pos_off, s_real)[:, :t]
    ob, s_new = _retention(p["qb"], p["kb"], p["vb"], s0)
    x_new = _outproj(x, oa, p["ga"], ob, p["gb"], w_out, g_post, gate[:, None, :])
    k_n = ka.reshape(b, t, N_KV_A, HD_A)
    v_n = va.reshape(b, t, N_KV_A, HD_A)
    return x_new, (k_n, v_n, kiw[..., :IDX_DIM], s_new)


def kernel(x_prompt, x_sample, cache_k_a, cache_v_a, cache_kidx_a, state_s_b, c_prompt, c_sample,
           w_ada, b_ada, g_pre, w_in, w_out, g_post):
    depth = w_in.shape[0]
    nb = x_prompt.shape[0]
    past_len = cache_k_a.shape[2]
    tabs_p = _rope_tables(jnp.arange(x_prompt.shape[1], dtype=jnp.int32))
    tabs_s = _rope_tables(past_len + jnp.arange(x_sample.shape[1], dtype=jnp.int32))
    mod = _modulation(jnp.concatenate([c_prompt, c_sample], axis=0), w_ada, b_ada)
    hp, hs = x_prompt, x_sample
    outs_p, outs_s = [], []
    for l in range(depth):
        wp, wr = _prep_w_in(w_in[l])
        hp, new_p = _layer(hp, mod[l, :nb], 0, tabs_p, wp, wr, g_pre[l], w_out[l], g_post[l], None)
        outs_p.append(new_p)
        hs, new_s = _layer(hs, mod[l, nb:], past_len, tabs_s, wp, wr, g_pre[l], w_out[l], g_post[l],
                           (cache_k_a[l], cache_v_a[l], cache_kidx_a[l], state_s_b[l]))
        outs_s.append(new_s)
    stack = lambda outs, i: jnp.stack([o[i] for o in outs])
    return (hp, hs,
            stack(outs_p, 0), stack(outs_p, 1), stack(outs_p, 2), stack(outs_p, 3),
            stack(outs_s, 0), stack(outs_s, 1), stack(outs_s, 2), stack(outs_s, 3))
```

```python
import functools
import math

import jax
import jax.numpy as jnp
from jax import lax
from jax.experimental import pallas as pl
from jax.experimental.pallas import tpu as pltpu

D_MODEL = 1024
CHUNK = 64
W_A = 512
HD_A = 64
N_HEADS_A = 8
N_KV_A = 2
N_GROUP_A = N_HEADS_A // N_KV_A
N_IDX = 4
IDX_DIM = 64
MAX_TOPK = 256
HD_B = 128
N_HEADS_B = 4
W_B = 512
ROPE_THETA = 10000.0
EPS = 1e-6
IDX_SCALE = (IDX_DIM ** -0.5) * (N_IDX ** -0.5)
LOG2E = math.log2(math.e)

LANES = 128
PACK16 = 16
VMEM_LIMIT = 56 * 1024 * 1024
INT_MIN = -(2 ** 31)
NEG = -1e30

QB = LANES
SEL_BLK = 512
ATT_BLK = 256
VROWS = HD_A + PACK16

F32 = jnp.float32
BF16 = jnp.bfloat16
I16 = jnp.int16
I32 = jnp.int32

_SEGS = (
    ("qa", W_A, "a"),
    ("ka", N_KV_A * HD_A, "a"),
    ("va", N_KV_A * HD_A, None),
    ("ga", W_A, None),
    ("qi", N_IDX * IDX_DIM, "a"),
    ("kiw", LANES, "k"),
    ("qb", W_B, "b"),
    ("kb", W_B, "b"),
    ("vb", W_B, None),
    ("gb", W_B, None),
)
_W_COLS = sum(s[1] for s in _SEGS)
_OUT_DTYPES = {"qa": BF16, "ka": F32, "va": F32, "ga": BF16, "qi": BF16, "kiw": F32,
               "qb": BF16, "kb": BF16, "vb": BF16, "gb": BF16}


def _silu(x):
    return x / (1.0 + jnp.exp(-x))


def _fold(op, x):
    parts = [x[i] for i in range(x.shape[0])]
    while len(parts) > 1:
        parts = [op(parts[i], parts[i + 1]) for i in range(0, len(parts) - 1, 2)] + parts[len(parts) & ~1:]
    return parts[0]


def _mod_kernel(c_ref, w_ref, b_ref, o_ref):
    c = c_ref[...]
    s = _silu(c)
    o_ref[0] = jnp.dot(s.astype(BF16), w_ref[0].astype(BF16), preferred_element_type=F32) + b_ref[0]


def _modulation(c_all, w_ada, b_ada):
    depth = w_ada.shape[0]
    n = c_all.shape[0]
    nblk = 3
    return pl.pallas_call(
        _mod_kernel,
        grid=(depth, nblk),
        in_specs=[
            pl.BlockSpec((n, D_MODEL), lambda l, j: (0, 0)),
            pl.BlockSpec((1, D_MODEL, D_MODEL), lambda l, j: (l, 0, j)),
            pl.BlockSpec((1, 1, D_MODEL), lambda l, j: (l, 0, j)),
        ],
        out_specs=pl.BlockSpec((1, n, D_MODEL), lambda l, j: (l, 0, j)),
        out_shape=jax.ShapeDtypeStruct((depth, n, 3 * D_MODEL), F32),
        compiler_params=pltpu.CompilerParams(dimension_semantics=("arbitrary", "arbitrary"),
                                             vmem_limit_bytes=VMEM_LIMIT),
        name="modulation",
    )(c_all, w_ada, b_ada.reshape(depth, 1, 3 * D_MODEL))


def _rope_slab(p, kind, tabs):
    if kind == "b":
        cos, sin_s = tabs["b"]
        return p * cos[...] + pltpu.roll(p, HD_B // 2, axis=1) * sin_s[...]
    cos, sin_lo, sin_hi = tabs[kind]
    return (p * cos[...] + pltpu.roll(p, LANES - HD_A // 2, axis=1) * sin_lo[...]
            + pltpu.roll(p, HD_A // 2, axis=1) * sin_hi[...])


def _inproj_kernel(x_ref, sc_ref, sh_ref, g_ref, w_ref,
                   ca_ref, sal_ref, sah_ref, ck_ref, skl_ref, skh_ref, cb_ref, sbs_ref, *out_refs):
    x = x_ref[0]
    ms = jnp.mean(x * x, axis=-1, keepdims=True)
    y = x * lax.rsqrt(ms + EPS) * g_ref[...]
    h = y * sc_ref[0] + sh_ref[0]
    hb = h.astype(BF16)
    tabs = {"a": (ca_ref, sal_ref, sah_ref), "k": (ck_ref, skl_ref, skh_ref), "b": (cb_ref, sbs_ref)}
    c0 = 0
    for (name, width, kind), o_ref in zip(_SEGS, out_refs):
        step = 2 * LANES if width % (2 * LANES) == 0 else LANES
        for off in range(0, width, step):
            p = jnp.dot(hb, w_ref[:, c0 + off:c0 + off + step], preferred_element_type=F32)
            for s in range(step // LANES):
                ps = p[:, s * LANES:(s + 1) * LANES]
                if kind is not None:
                    ps = _rope_slab(ps, kind, tabs)
                if name == "kb":
                    ps = ps * (HD_B ** -0.5)
                if name in ("ga", "gb"):
                    ps = _silu(ps)
                o_ref[0, :, off + s * LANES:off + (s + 1) * LANES] = ps.astype(o_ref.dtype)
        c0 += width


def _prep_w_in(w):
    sizes = (W_A, N_KV_A * HD_A, N_KV_A * HD_A, W_A, N_IDX * IDX_DIM, IDX_DIM, N_IDX, W_B, W_B, W_B, W_B)
    parts = []
    o = 0
    for n in sizes:
        parts.append(w[:, o:o + n])
        o += n
    q_a, k_a, v_a, g_a, q_i, k_i, w_i, q_b, k_b, v_b, g_b = parts
    kk = w.shape[0]
    q_a = q_a * ((HD_A ** -0.5) * LOG2E)
    kiw = jnp.concatenate([k_i, w_i, jnp.zeros((kk, LANES - IDX_DIM - N_IDX), w.dtype)], axis=1)
    segs = {"qa": q_a, "ka": k_a, "va": v_a, "ga": g_a, "qi": q_i, "kiw": kiw,
            "qb": q_b, "kb": k_b, "vb": v_b, "gb": g_b}
    return jnp.concatenate([segs[s[0]] for s in _SEGS], axis=1).astype(BF16)


def _rope_tables(pos):
    posf = pos.astype(F32)[:, None]
    lane = jnp.arange(LANES)[None, :]

    def tab(d):
        half = d // 2
        inv = ROPE_THETA ** (-jnp.arange(half, dtype=F32) * 2.0 / d)
        ang = posf * inv[None, :]
        reps = LANES // half
        return jnp.tile(jnp.cos(ang), (1, reps)), jnp.tile(jnp.sin(ang), (1, reps)), (lane % d) < half

    ca, sa, lo_a = tab(HD_A)
    sal, sah = jnp.where(lo_a, -sa, 0.0), jnp.where(lo_a, 0.0, sa)
    in_k = lane < IDX_DIM
    ck, skl, skh = jnp.where(in_k, ca, 1.0), jnp.where(in_k, sal, 0.0), jnp.where(in_k, sah, 0.0)
    cb, sb, lo_b = tab(HD_B)
    return ca, sal, sah, ck, skl, skh, cb, jnp.where(lo_b, -sb, sb)


def _row_tile(t):
    return min(512, t)


def _inproj(x, scale1p, shift, g_pre, wp, tabs):
    b, t, _ = x.shape
    tm = _row_tile(t)
    nt = t // tm
    const = lambda i, j: (0, 0)
    tab_spec = pl.BlockSpec((tm, LANES), lambda i, j: (j, 0))
    in_specs = [
        pl.BlockSpec((1, tm, D_MODEL), lambda i, j: (i, j, 0)),
        pl.BlockSpec((1, 1, D_MODEL), lambda i, j: (i, 0, 0)),
        pl.BlockSpec((1, 1, D_MODEL), lambda i, j: (i, 0, 0)),
        pl.BlockSpec((1, D_MODEL), const),
        pl.BlockSpec((D_MODEL, _W_COLS), const, pipeline_mode=pl.Buffered(1)),
    ] + [tab_spec] * len(tabs)
    out_specs = [pl.BlockSpec((1, tm, s[1]), lambda i, j: (i, j, 0)) for s in _SEGS]
    out_shape = [jax.ShapeDtypeStruct((b, t, s[1]), _OUT_DTYPES[s[0]]) for s in _SEGS]
    outs = pl.pallas_call(
        _inproj_kernel,
        grid=(b, nt),
        in_specs=in_specs,
        out_specs=out_specs,
        out_shape=out_shape,
        compiler_params=pltpu.CompilerParams(dimension_semantics=("arbitrary", "arbitrary"),
                                             vmem_limit_bytes=VMEM_LIMIT),
        name="inproj",
    )(x, scale1p, shift, g_pre.reshape(1, D_MODEL), wp, *tabs)
    return dict(zip([s[0] for s in _SEGS], outs))


def _attn_kernel(qa_ref, qi_ref, wq_ref, kk_ref, kiw_ref, vt_ref, tri_ref, o_ref,
                 keys_ref, planes_ref, alive_ref, bias_ref, qt_ref, qit_ref, m_ref, acc_ref, sb_ref, x_ref,
                 *, pos_off, topk, s_real):
    j = pl.program_id(1)
    npair = N_HEADS_A // 2

    qlane = lax.broadcasted_iota(I32, (1, QB), 1)
    pos = pos_off + j * QB + qlane
    lim = jnp.minimum((lax.shift_right_logical(pos, 6) + 1) * CHUNK, s_real)
    last_pos = pos_off + (j + 1) * QB - 1
    lim_max = jnp.minimum((lax.shift_right_logical(last_pos, 6) + 1) * CHUNK, s_real)
    n_sel = lax.div(lim_max + (SEL_BLK - 1), SEL_BLK)
    n_att = n_sel * (SEL_BLK // ATT_BLK)

    zeros_half = jnp.zeros((HD_A, QB), BF16)
    for s in range(N_HEADS_A // 2):
        t = qa_ref[0, :, s * LANES:(s + 1) * LANES].astype(F32).T.astype(BF16)
        n = (2 * s) // N_GROUP_A
        for c in range(2):
            cols = slice((2 * s + c) * LANES, (2 * s + c + 1) * LANES)
            qt_ref[n * HD_A:(n + 1) * HD_A, cols] = t[c * HD_A:(c + 1) * HD_A, :]
            qt_ref[(1 - n) * HD_A:(2 - n) * HD_A, cols] = zeros_half
    for s in range(N_IDX // 2):
        t = qi_ref[0, :, s * LANES:(s + 1) * LANES].astype(F32).T.astype(BF16)
        for c in range(2):
            cols = slice((2 * s + c) * LANES, (2 * s + c + 1) * LANES)
            qit_ref[:IDX_DIM, cols] = t[c * IDX_DIM:(c + 1) * IDX_DIM, :]
            qit_ref[IDX_DIM:, cols] = zeros_half
    wt = wq_ref[0].T
    w_idx = [wt[IDX_DIM + h:IDX_DIM + h + 1, :] * IDX_SCALE for h in range(N_IDX)]

    def score_dots(off, slot):
        kblk = kiw_ref[0, pl.ds(off, ATT_BLK), :]
        for m in range(N_IDX // 2):
            x = jnp.dot(kblk, qit_ref[:, 2 * m * LANES:(2 * m + 2) * LANES], preferred_element_type=F32)
            x_ref[slot, 2 * m] = x[:, :LANES]
            x_ref[slot, 2 * m + 1] = x[:, LANES:]

    def score_keys(off, slot):
        sc = None
        for h in range(N_IDX):
            term = jnp.maximum(x_ref[slot, h], 0.0) * w_idx[h]
            sc = term if sc is None else sc + term
        bits = pltpu.bitcast(sc, I32)
        keys_ref[pl.ds(off, ATT_BLK), :] = bits ^ (lax.shift_right_arithmetic(bits, 31) & 0x7FFFFFFF)

    x_ref[1] = jnp.zeros(x_ref.shape[1:], F32)

    def score_two_blocks(i, carry):
        off0 = pl.multiple_of(i * SEL_BLK, SEL_BLK)
        off1 = pl.multiple_of(off0 + ATT_BLK, ATT_BLK)
        off_prev = pl.multiple_of(jnp.maximum(off0 - ATT_BLK, 0), ATT_BLK)
        score_dots(off0, 0)
        score_keys(off_prev, 1)
        score_dots(off1, 1)
        score_keys(off0, 0)
        return carry

    lax.fori_loop(0, n_sel, score_two_blocks, 0)
    score_keys(pl.multiple_of(n_sel * SEL_BLK - ATT_BLK, ATT_BLK), 1)

    n_blk = n_sel * (SEL_BLK // ATT_BLK)
    first_pos = pos_off + j * QB
    lim_min = jnp.minimum((lax.shift_right_logical(first_pos, 6) + 1) * CHUNK, s_real)

    def mask_blk(i, carry):
        off = pl.multiple_of(i * ATT_BLK, ATT_BLK)
        kpos = off + lax.broadcasted_iota(I32, (ATT_BLK, QB), 0)
        keys_ref[pl.ds(off, ATT_BLK), :] = jnp.where(kpos < lim, keys_ref[pl.ds(off, ATT_BLK), :], INT_MIN)
        return carry

    lax.fori_loop(lax.div(lim_min, ATT_BLK), n_blk, mask_blk, 0)

    vregs = ATT_BLK // 8

    def plane_blk(i, carry):
        off = pl.multiple_of(i * ATT_BLK, ATT_BLK)
        blk = keys_ref[pl.ds(off, ATT_BLK), :]
        a = [blk[8 * r:8 * r + 8, :] ^ INT_MIN for r in range(vregs)]
        for j, msk in ((16, 0x0000FFFF), (8, 0x00FF00FF), (4, 0x0F0F0F0F), (2, 0x33333333), (1, 0x55555555)):
            for k in range(vregs):
                if k & j == 0:
                    t = (a[k] ^ lax.shift_right_logical(a[k + j], j)) & msk
                    a[k] = a[k] ^ t
                    a[k + j] = a[k + j] ^ lax.shift_left(t, j)
        for r in range(vregs):
            planes_ref[r, i] = a[r]
        return carry

    lax.fori_loop(0, n_blk, plane_blk, 0)
    alive_ref[...] = jnp.full(alive_ref.shape, -1, I32)

    def count_ones(p_cur, p_prev, flip):
        def blk(i, acc):
            for i_blk in (2 * i, 2 * i + 1):
                alive = alive_ref[i_blk]
                if p_prev is not None:
                    alive = alive & (planes_ref[p_prev, i_blk] ^ flip)
                    alive_ref[i_blk] = alive
                acc = acc + lax.population_count(alive & planes_ref[p_cur, i_blk])
            return acc

        acc = lax.fori_loop(0, n_sel, blk, jnp.zeros((8, QB), I32))
        return jnp.sum(acc, axis=0, keepdims=True)

    def decide(p, ones, st):
        thr_u, rem, _ = st
        take = ones >= rem
        bit = lax.shift_left(jnp.int32(1), 31 - p)
        return (jnp.where(take, thr_u | bit, thr_u), jnp.where(take, rem, rem - ones),
                jnp.where(take, 0, -1))

    zero = jnp.zeros((1, QB), I32)
    st = decide(0, count_ones(0, None, None), (zero, zero + topk, zero))
    st = lax.fori_loop(1, 32, lambda p, s: decide(p, count_ones(p, p - 1, s[2]), s), st)
    thr_u, need, flip = st

    def eq_blk(i, acc):
        return acc + lax.population_count(alive_ref[i] & (planes_ref[31, i] ^ flip))

    n_eq = jnp.sum(lax.fori_loop(0, n_blk, eq_blk, jnp.zeros((8, QB), I32)), axis=0, keepdims=True)
    thr = jnp.maximum(thr_u ^ INT_MIN, INT_MIN + 1)
    tie_cut = jnp.max(jnp.where((n_eq > need) & (thr_u != 0), 1, 0)) > 0

    @pl.when(jnp.logical_not(tie_cut))
    def _():
        def blk(i, carry):
            off = pl.multiple_of(i * ATT_BLK, ATT_BLK)
            key = keys_ref[pl.ds(off, ATT_BLK), :]
            bias_ref[pl.ds(off, ATT_BLK), :] = jnp.where(key >= thr, 0.0, NEG).astype(BF16)
            return carry

        lax.fori_loop(0, n_att, blk, 0)

    @pl.when(tie_cut)
    def _():
        need_f = need.astype(F32)

        def blk(i, run):
            off = pl.multiple_of(i * ATT_BLK, ATT_BLK)
            key = keys_ref[pl.ds(off, ATT_BLK), :]
            eq = key == thr
            pre = jnp.dot(tri_ref[...], jnp.where(eq, 1.0, 0.0).astype(BF16), preferred_element_type=F32)
            keep_eq = (pre + run) <= need_f
            bias = jnp.where(key > thr, 0.0, jnp.where(eq, jnp.where(keep_eq, 0.0, NEG), NEG))
            bias_ref[pl.ds(off, ATT_BLK), :] = bias.astype(BF16)
            return run + pre[ATT_BLK - 1:ATT_BLK, :]

        lax.fori_loop(0, n_att, blk, jnp.zeros((1, QB), F32))

    m_ref[...] = jnp.full(m_ref.shape, NEG, F32)
    acc_ref[...] = jnp.zeros(acc_ref.shape, F32)

    def stage_logits(off, slot):
        kblk = kk_ref[0, pl.ds(off, ATT_BLK), :]
        bias = bias_ref[pl.ds(off, ATT_BLK), :]
        bias2 = jnp.concatenate([bias, bias], axis=1)
        parts = []
        for m in range(npair):
            s = jnp.dot(kblk, qt_ref[:, 2 * m * LANES:(2 * m + 2) * LANES], preferred_element_type=F32)
            sb = s.astype(BF16) + bias2
            sb_ref[slot, m] = sb
            parts.append(_fold(jnp.maximum, sb.reshape(ATT_BLK // PACK16, PACK16, 2 * QB)))
        return parts

    def stage_probs(parts, slot):
        alphas = []
        for m in range(npair):
            m_prev = m_ref[m]
            m_new = jnp.maximum(m_prev, jnp.max(parts[m].astype(F32), axis=0, keepdims=True))
            alphas.append(jnp.exp2(m_prev - m_new))
            sb_ref[slot, m] = jnp.exp2(sb_ref[slot, m] - m_new.astype(BF16))
            m_ref[m] = m_new
        return alphas

    def stage_values(off, slot, alphas):
        for m in range(npair):
            n = (2 * m) // N_GROUP_A
            pv = jnp.dot(vt_ref[0, n * VROWS:(n + 1) * VROWS, pl.ds(off, ATT_BLK)], sb_ref[slot, m],
                         preferred_element_type=F32)
            acc_ref[m] = alphas[m] * acc_ref[m] + pv

    sb_ref[1] = jnp.full(sb_ref.shape[1:], -jnp.inf, BF16)

    def attn_two_blocks(i, parts_prev):
        off0 = pl.multiple_of(i * SEL_BLK, SEL_BLK)
        off1 = pl.multiple_of(off0 + ATT_BLK, ATT_BLK)
        off_prev = pl.multiple_of(jnp.maximum(off0 - ATT_BLK, 0), ATT_BLK)
        parts0 = stage_logits(off0, 0)
        alphas_prev = stage_probs(parts_prev, 1)
        stage_values(off_prev, 1, alphas_prev)
        alphas0 = stage_probs(parts0, 0)
        parts1 = stage_logits(off1, 1)
        stage_values(off0, 0, alphas0)
        return tuple(parts1)

    neg_parts = tuple(jnp.full((PACK16, 2 * QB), NEG, BF16) for _ in range(npair))
    parts_last = lax.fori_loop(0, n_sel, attn_two_blocks, neg_parts)
    alphas_last = stage_probs(parts_last, 1)
    stage_values(pl.multiple_of(n_sel * SEL_BLK - ATT_BLK, ATT_BLK), 1, alphas_last)

    heads = []
    for m in range(npair):
        a = acc_ref[m]
        o = a[:HD_A, :] / a[HD_A:HD_A + 1, :]
        heads += [o[:, :QB], o[:, QB:]]
    o_ref[0] = jnp.concatenate(heads, axis=0).T


def _attention(qa, qi, kiw_q, kk, kiw_k, vt, pos_off, s_real):
    b, t, _ = qa.shape
    s_pad = kk.shape[1]
    topk = min(MAX_TOPK, s_real // 4)
    idx = jnp.arange(ATT_BLK)
    tri = (idx[None, :] <= idx[:, None]).astype(BF16)
    kern = functools.partial(_attn_kernel, pos_off=pos_off, topk=topk, s_real=s_real)
    npair = N_HEADS_A // 2
    return pl.pallas_call(
        kern,
        grid=(b, t // QB),
        in_specs=[
            pl.BlockSpec((1, QB, W_A), lambda i, j: (i, j, 0)),
            pl.BlockSpec((1, QB, N_IDX * IDX_DIM), lambda i, j: (i, j, 0)),
            pl.BlockSpec((1, QB, LANES), lambda i, j: (i, j, 0)),
            pl.BlockSpec((1, s_pad, LANES), lambda i, j: (i, 0, 0)),
            pl.BlockSpec((1, s_pad, LANES), lambda i, j: (i, 0, 0)),
            pl.BlockSpec((1, N_KV_A * VROWS, s_pad), lambda i, j: (i, 0, 0)),
            pl.BlockSpec((ATT_BLK, ATT_BLK), lambda i, j: (0, 0)),
        ],
        out_specs=pl.BlockSpec((1, QB, W_A), lambda i, j: (i, j, 0)),
        out_shape=jax.ShapeDtypeStruct((b, t, W_A), F32),
        scratch_shapes=[
            pltpu.VMEM((s_pad, QB), I32),
            pltpu.VMEM((ATT_BLK // 8, s_pad // ATT_BLK, 8, QB), I32),
            pltpu.VMEM((s_pad // ATT_BLK, 8, QB), I32),
            pltpu.VMEM((s_pad, QB), BF16),
            pltpu.VMEM((LANES, N_HEADS_A * LANES), BF16),
            pltpu.VMEM((LANES, N_IDX * LANES), BF16),
            pltpu.VMEM((npair, 1, 2 * QB), F32),
            pltpu.VMEM((npair, VROWS, 2 * QB), F32),
            pltpu.VMEM((2, npair, ATT_BLK, 2 * QB), BF16),
            pltpu.VMEM((2, N_IDX, ATT_BLK, QB), F32),
        ],
        compiler_params=pltpu.CompilerParams(dimension_semantics=("arbitrary", "arbitrary"),
                                             vmem_limit_bytes=VMEM_LIMIT),
        name="dsa_attention",
    )(qa, qi, kiw_q, kk, kiw_k, vt, tri)


def _attn_operands(ka, va, kiw, s_pad):
    b, s, _ = ka.shape
    pad = ((0, 0), (0, s_pad - s), (0, 0))
    kk = jnp.pad(ka.astype(BF16), pad)
    kiw_k = jnp.pad(kiw.astype(BF16), pad)
    vb = jnp.pad(va.astype(BF16), pad).reshape(b, s_pad, N_KV_A, HD_A)
    ones = jnp.ones((b, s_pad, N_KV_A, VROWS - HD_A), BF16)
    vaug = jnp.concatenate([vb, ones], axis=-1).reshape(b, s_pad, N_KV_A * VROWS)
    return kk, kiw_k, jnp.swapaxes(vaug, 1, 2)


def _pad_rows(a, t_pad):
    return jnp.pad(a, ((0, 0), (0, t_pad - a.shape[1]), (0, 0)))


def _ret_kernel(q_ref, k_ref, v_ref, s0_ref, dm_ref, qd_ref, kd_ref, sd_ref, o_ref, sn_ref, s_scr,
                *, rt, rc):
    t = pl.program_id(1)

    @pl.when(t == 0)
    def _():
        s_scr[...] = s0_ref[0]

    for c in range(rt // rc):
        rows = slice(c * rc, (c + 1) * rc)
        for h in range(N_HEADS_B):
            cols = slice(h * HD_B, (h + 1) * HD_B)
            q = q_ref[0, rows, cols]
            k = k_ref[0, rows, cols]
            v = v_ref[0, rows, cols]
            att = lax.dot_general(q, k, (((1,), (1,)), ((), ())), preferred_element_type=F32) * dm_ref[h]
            s = s_scr[h]
            o = (jnp.dot(att.astype(BF16), v, preferred_element_type=F32)
                 + jnp.dot(q, s.astype(BF16), preferred_element_type=F32) * qd_ref[h])
            kd = (k.astype(F32) * kd_ref[h]).astype(BF16)
            s_scr[h] = s * sd_ref[h] + lax.dot_general(kd, v, (((0,), (0,)), ((), ())),
                                                       preferred_element_type=F32)
            mu = jnp.mean(o, axis=-1, keepdims=True)
            d = o - mu
            var = jnp.mean(d * d, axis=-1, keepdims=True)
            o_ref[0, rows, cols] = d * lax.rsqrt(var + EPS)

    @pl.when(t == pl.num_programs(1) - 1)
    def _():
        sn_ref[0] = s_scr[...]


def _retention(qb_, kb_, vb_, s0):
    b, t, _ = qb_.shape
    rt = min(512, t)
    rc = min(128, t)
    lg = jnp.log1p(-jnp.exp2(-5.0 - jnp.arange(N_HEADS_B, dtype=F32)))
    i = jnp.arange(rc, dtype=F32)
    diff = i[:, None] - i[None, :]
    dmask = jnp.exp(jnp.where(diff >= 0, diff[None] * lg[:, None, None], -jnp.inf))
    ones = jnp.ones((1, 1, HD_B), F32)
    q_dec = jnp.exp((i + 1.0)[None, :] * lg[:, None])[:, :, None] * ones
    k_dec = jnp.exp((rc - 1.0 - i)[None, :] * lg[:, None])[:, :, None] * ones
    s_dec = jnp.exp(rc * lg)[:, None, None] * jnp.ones((1, HD_B, HD_B), F32)
    blk = pl.BlockSpec((1, rt, W_B), lambda i_, j: (i_, j, 0))
    st = pl.BlockSpec((1, N_HEADS_B, HD_B, HD_B), lambda i_, j: (i_, 0, 0, 0))
    c3 = lambda i_, j: (0, 0, 0)
    kern = functools.partial(_ret_kernel, rt=rt, rc=rc)
    return pl.pallas_call(
        kern,
        grid=(b, t // rt),
        in_specs=[blk, blk, blk, st,
                  pl.BlockSpec((N_HEADS_B, rc, rc), c3),
                  pl.BlockSpec((N_HEADS_B, rc, HD_B), c3),
                  pl.BlockSpec((N_HEADS_B, rc, HD_B), c3),
                  pl.BlockSpec((N_HEADS_B, HD_B, HD_B), c3)],
        out_specs=[blk, st],
        out_shape=[jax.ShapeDtypeStruct((b, t, W_B), F32),
                   jax.ShapeDtypeStruct((b, N_HEADS_B, HD_B, HD_B), F32)],
        scratch_shapes=[pltpu.VMEM((N_HEADS_B, HD_B, HD_B), F32)],
        compiler_params=pltpu.CompilerParams(dimension_semantics=("arbitrary", "arbitrary"),
                                             vmem_limit_bytes=VMEM_LIMIT),
        name="retention",
    )(qb_, kb_, vb_, s0, dmask, q_dec, k_dec, s_dec)


def _outproj_kernel(x_ref, oa_ref, ga_ref, ob_ref, gb_ref, w_ref, g_ref, gate_ref, o_ref):
    ma = (oa_ref[0] * ga_ref[0].astype(F32)).astype(BF16)
    mb = (ob_ref[0] * gb_ref[0].astype(F32)).astype(BF16)
    y = (jnp.dot(ma, w_ref[:W_A, :], preferred_element_type=F32)
         + jnp.dot(mb, w_ref[W_A:, :], preferred_element_type=F32))
    ms = jnp.mean(y * y, axis=-1, keepdims=True)
    yn = y * lax.rsqrt(ms + EPS) * g_ref[...]
    o_ref[0] = x_ref[0] + gate_ref[0] * yn


def _outproj(x, oa, ga, ob, gb, w_out, g_post, gate):
    b, t, _ = x.shape
    tm = _row_tile(t)
    row = lambda w: pl.BlockSpec((1, tm, w), lambda i, j: (i, j, 0))
    return pl.pallas_call(
        _outproj_kernel,
        grid=(b, t // tm),
        in_specs=[row(D_MODEL), row(W_A), row(W_A), row(W_B), row(W_B),
                  pl.BlockSpec((D_MODEL, D_MODEL), lambda i, j: (0, 0)),
                  pl.BlockSpec((1, D_MODEL), lambda i, j: (0, 0)),
                  pl.BlockSpec((1, 1, D_MODEL), lambda i, j: (i, 0, 0))],
        out_specs=row(D_MODEL),
        out_shape=jax.ShapeDtypeStruct((b, t, D_MODEL), F32),
        compiler_params=pltpu.CompilerParams(dimension_semantics=("arbitrary", "arbitrary"),
                                             vmem_limit_bytes=VMEM_LIMIT),
        name="outproj",
    )(x, oa, ga, ob, gb, w_out.astype(BF16), g_post.reshape(1, D_MODEL), gate)


def _layer(x, mod, pos_off, tabs, wp, g_pre, w_out, g_post, past):
    b, t, _ = x.shape
    shift, scale, gate = jnp.split(mod, 3, axis=-1)
    p = _inproj(x, (1.0 + scale)[:, None, :], shift[:, None, :], g_pre, wp, tabs)
    ka, va, kiw = p["ka"], p["va"], p["kiw"]
    if past is None:
        k_all, v_all, kiw_all = ka, va, kiw
        s0 = jnp.zeros((b, N_HEADS_B, HD_B, HD_B), F32)
    else:
        pk, pv, pki, s0 = past
        plen = pk.shape[1]
        k_all = jnp.concatenate([pk.reshape(b, plen, N_KV_A * HD_A), ka], axis=1)
        v_all = jnp.concatenate([pv.reshape(b, plen, N_KV_A * HD_A), va], axis=1)
        pki = jnp.pad(pki, ((0, 0), (0, 0), (0, LANES - IDX_DIM)))
        kiw_all = jnp.concatenate([pki, kiw], axis=1)
    s_real = k_all.shape[1]
    s_pad = -(-s_real // SEL_BLK) * SEL_BLK
    t_pad = -(-t // QB) * QB
    kk, kiw_k, vt = _attn_operands(k_all, v_all, kiw_all, s_pad)
    oa = _attention(_pad_rows(p["qa"], t_pad), _pad_rows(p["qi"], t_pad), _pad_rows(kiw, t_pad),
                    kk, kiw_k, vt, pos_off, s_real)[:, :t]
    ob, s_new = _retention(p["qb"], p["kb"], p["vb"], s0)
    x_new = _outproj(x, oa, p["ga"], ob, p["gb"], w_out, g_post, gate[:, None, :])
    k_n = ka.reshape(b, t, N_KV_A, HD_A)
    v_n = va.reshape(b, t, N_KV_A, HD_A)
    return x_new, (k_n, v_n, kiw[..., :IDX_DIM], s_new)


def kernel(x_prompt, x_sample, cache_k_a, cache_v_a, cache_kidx_a, state_s_b, c_prompt, c_sample,
           w_ada, b_ada, g_pre, w_in, w_out, g_post):
    depth = w_in.shape[0]
    nb = x_prompt.shape[0]
    past_len = cache_k_a.shape[2]
    tabs_p = _rope_tables(jnp.arange(x_prompt.shape[1], dtype=jnp.int32))
    tabs_s = _rope_tables(past_len + jnp.arange(x_sample.shape[1], dtype=jnp.int32))
    mod = _modulation(jnp.concatenate([c_prompt, c_sample], axis=0), w_ada, b_ada)
    hp, hs = x_prompt, x_sample
    outs_p, outs_s = [], []
    for l in range(depth):
        wp = _prep_w_in(w_in[l])
        hp, new_p = _layer(hp, mod[l, :nb], 0, tabs_p, wp, g_pre[l], w_out[l], g_post[l], None)
        outs_p.append(new_p)
        hs, new_s = _layer(hs, mod[l, nb:], past_len, tabs_s, wp, g_pre[l], w_out[l], g_post[l],
                           (cache_k_a[l], cache_v_a[l], cache_kidx_a[l], state_s_b[l]))
        outs_s.append(new_s)
    stack = lambda outs, i: jnp.stack([o[i] for o in outs])
    return (hp, hs,
            stack(outs_p, 0), stack(outs_p, 1), stack(outs_p, 2), stack(outs_p, 3),
            stack(outs_s, 0), stack(outs_s, 1), stack(outs_s, 2), stack(outs_s, 3))
```

```python
import functools
import math

import jax
import jax.numpy as jnp
from jax import lax
from jax.experimental import pallas as pl
from jax.experimental.pallas import tpu as pltpu

D_MODEL = 1024
CHUNK = 64
W_A = 512
HD_A = 64
N_HEADS_A = 8
N_KV_A = 2
N_GROUP_A = N_HEADS_A // N_KV_A
N_IDX = 4
IDX_DIM = 64
MAX_TOPK = 256
HD_B = 128
N_HEADS_B = 4
W_B = 512
ROPE_THETA = 10000.0
EPS = 1e-6
IDX_SCALE = (IDX_DIM ** -0.5) * (N_IDX ** -0.5)
LOG2E = math.log2(math.e)

LANES = 128
PACK16 = 16
VMEM_LIMIT = 56 * 1024 * 1024
INT_MIN = -(2 ** 31)
NEG = -1e30

QB = LANES
SEL_BLK = 512
ATT_BLK = 256
SEL_GRP = 8
VROWS = HD_A + PACK16

F32 = jnp.float32
BF16 = jnp.bfloat16
I16 = jnp.int16
I32 = jnp.int32

_SEGS = (
    ("qa", W_A, "a"),
    ("ka", N_KV_A * HD_A, "a"),
    ("va", N_KV_A * HD_A, None),
    ("ga", W_A, None),
    ("qi", N_IDX * IDX_DIM, "a"),
    ("kiw", LANES, "k"),
    ("qb", W_B, "b"),
    ("kb", W_B, "b"),
    ("vb", W_B, None),
    ("gb", W_B, None),
)
_W_COLS = sum(s[1] for s in _SEGS)
_OUT_DTYPES = {"qa": BF16, "ka": F32, "va": F32, "ga": BF16, "qi": BF16, "kiw": F32,
               "qb": BF16, "kb": BF16, "vb": BF16, "gb": BF16}


def _silu(x):
    return x / (1.0 + jnp.exp(-x))


def _fold(op, x):
    parts = [x[i] for i in range(x.shape[0])]
    while len(parts) > 1:
        parts = [op(parts[i], parts[i + 1]) for i in range(0, len(parts) - 1, 2)] + parts[len(parts) & ~1:]
    return parts[0]


def _mod_kernel(c_ref, w_ref, b_ref, o_ref):
    c = c_ref[...]
    s = _silu(c)
    o_ref[0] = jnp.dot(s.astype(BF16), w_ref[0].astype(BF16), preferred_element_type=F32) + b_ref[0]


def _modulation(c_all, w_ada, b_ada):
    depth = w_ada.shape[0]
    n = c_all.shape[0]
    nblk = 3
    return pl.pallas_call(
        _mod_kernel,
        grid=(depth, nblk),
        in_specs=[
            pl.BlockSpec((n, D_MODEL), lambda l, j: (0, 0)),
            pl.BlockSpec((1, D_MODEL, D_MODEL), lambda l, j: (l, 0, j)),
            pl.BlockSpec((1, 1, D_MODEL), lambda l, j: (l, 0, j)),
        ],
        out_specs=pl.BlockSpec((1, n, D_MODEL), lambda l, j: (l, 0, j)),
        out_shape=jax.ShapeDtypeStruct((depth, n, 3 * D_MODEL), F32),
        compiler_params=pltpu.CompilerParams(dimension_semantics=("arbitrary", "arbitrary"),
                                             vmem_limit_bytes=VMEM_LIMIT),
        name="modulation",
    )(c_all, w_ada, b_ada.reshape(depth, 1, 3 * D_MODEL))


def _rope_slab(p, kind, tabs):
    if kind == "b":
        cos, sin_s = tabs["b"]
        return p * cos[...] + pltpu.roll(p, HD_B // 2, axis=1) * sin_s[...]
    cos, sin_lo, sin_hi = tabs[kind]
    return (p * cos[...] + pltpu.roll(p, LANES - HD_A // 2, axis=1) * sin_lo[...]
            + pltpu.roll(p, HD_A // 2, axis=1) * sin_hi[...])


def _inproj_kernel(x_ref, sc_ref, sh_ref, g_ref, w_ref,
                   ca_ref, sal_ref, sah_ref, ck_ref, skl_ref, skh_ref, cb_ref, sbs_ref, *out_refs):
    x = x_ref[0]
    ms = jnp.mean(x * x, axis=-1, keepdims=True)
    y = x * lax.rsqrt(ms + EPS) * g_ref[...]
    h = y * sc_ref[0] + sh_ref[0]
    hb = h.astype(BF16)
    tabs = {"a": (ca_ref, sal_ref, sah_ref), "k": (ck_ref, skl_ref, skh_ref), "b": (cb_ref, sbs_ref)}
    c0 = 0
    for (name, width, kind), o_ref in zip(_SEGS, out_refs):
        step = 2 * LANES if width % (2 * LANES) == 0 else LANES
        for off in range(0, width, step):
            p = jnp.dot(hb, w_ref[:, c0 + off:c0 + off + step], preferred_element_type=F32)
            for s in range(step // LANES):
                ps = p[:, s * LANES:(s + 1) * LANES]
                if kind is not None:
                    ps = _rope_slab(ps, kind, tabs)
                if name == "kb":
                    ps = ps * (HD_B ** -0.5)
                if name in ("ga", "gb"):
                    ps = _silu(ps)
                o_ref[0, :, off + s * LANES:off + (s + 1) * LANES] = ps.astype(o_ref.dtype)
        c0 += width


def _prep_w_in(w):
    sizes = (W_A, N_KV_A * HD_A, N_KV_A * HD_A, W_A, N_IDX * IDX_DIM, IDX_DIM, N_IDX, W_B, W_B, W_B, W_B)
    parts = []
    o = 0
    for n in sizes:
        parts.append(w[:, o:o + n])
        o += n
    q_a, k_a, v_a, g_a, q_i, k_i, w_i, q_b, k_b, v_b, g_b = parts
    kk = w.shape[0]
    q_a = q_a * ((HD_A ** -0.5) * LOG2E)
    kiw = jnp.concatenate([k_i, w_i, jnp.zeros((kk, LANES - IDX_DIM - N_IDX), w.dtype)], axis=1)
    segs = {"qa": q_a, "ka": k_a, "va": v_a, "ga": g_a, "qi": q_i, "kiw": kiw,
            "qb": q_b, "kb": k_b, "vb": v_b, "gb": g_b}
    return jnp.concatenate([segs[s[0]] for s in _SEGS], axis=1).astype(BF16)


def _rope_tables(pos):
    posf = pos.astype(F32)[:, None]
    lane = jnp.arange(LANES)[None, :]

    def tab(d):
        half = d // 2
        inv = ROPE_THETA ** (-jnp.arange(half, dtype=F32) * 2.0 / d)
        ang = posf * inv[None, :]
        reps = LANES // half
        return jnp.tile(jnp.cos(ang), (1, reps)), jnp.tile(jnp.sin(ang), (1, reps)), (lane % d) < half

    ca, sa, lo_a = tab(HD_A)
    sal, sah = jnp.where(lo_a, -sa, 0.0), jnp.where(lo_a, 0.0, sa)
    in_k = lane < IDX_DIM
    ck, skl, skh = jnp.where(in_k, ca, 1.0), jnp.where(in_k, sal, 0.0), jnp.where(in_k, sah, 0.0)
    cb, sb, lo_b = tab(HD_B)
    return ca, sal, sah, ck, skl, skh, cb, jnp.where(lo_b, -sb, sb)


def _row_tile(t):
    return min(512, t)


def _inproj(x, scale1p, shift, g_pre, wp, tabs):
    b, t, _ = x.shape
    tm = _row_tile(t)
    nt = t // tm
    const = lambda i, j: (0, 0)
    tab_spec = pl.BlockSpec((tm, LANES), lambda i, j: (j, 0))
    in_specs = [
        pl.BlockSpec((1, tm, D_MODEL), lambda i, j: (i, j, 0)),
        pl.BlockSpec((1, 1, D_MODEL), lambda i, j: (i, 0, 0)),
        pl.BlockSpec((1, 1, D_MODEL), lambda i, j: (i, 0, 0)),
        pl.BlockSpec((1, D_MODEL), const),
        pl.BlockSpec((D_MODEL, _W_COLS), const, pipeline_mode=pl.Buffered(1)),
    ] + [tab_spec] * len(tabs)
    out_specs = [pl.BlockSpec((1, tm, s[1]), lambda i, j: (i, j, 0)) for s in _SEGS]
    out_shape = [jax.ShapeDtypeStruct((b, t, s[1]), _OUT_DTYPES[s[0]]) for s in _SEGS]
    outs = pl.pallas_call(
        _inproj_kernel,
        grid=(b, nt),
        in_specs=in_specs,
        out_specs=out_specs,
        out_shape=out_shape,
        compiler_params=pltpu.CompilerParams(dimension_semantics=("arbitrary", "arbitrary"),
                                             vmem_limit_bytes=VMEM_LIMIT),
        name="inproj",
    )(x, scale1p, shift, g_pre.reshape(1, D_MODEL), wp, *tabs)
    return dict(zip([s[0] for s in _SEGS], outs))


def _attn_kernel(qa_ref, qi_ref, wq_ref, kk_ref, kiw_ref, vt_ref, tri_ref, o_ref,
                 keys_ref, planes_ref, alive_ref, bias_ref, qt_ref, qit_ref, m_ref, acc_ref, sb_ref, x_ref,
                 *, pos_off, topk, s_real):
    j = pl.program_id(1)
    npair = N_HEADS_A // 2

    qlane = lax.broadcasted_iota(I32, (1, QB), 1)
    pos = pos_off + j * QB + qlane
    lim = jnp.minimum((lax.shift_right_logical(pos, 6) + 1) * CHUNK, s_real)
    last_pos = pos_off + (j + 1) * QB - 1
    lim_max = jnp.minimum((lax.shift_right_logical(last_pos, 6) + 1) * CHUNK, s_real)
    n_sel = lax.div(lim_max + (SEL_BLK - 1), SEL_BLK)
    n_att = n_sel * (SEL_BLK // ATT_BLK)

    zeros_half = jnp.zeros((HD_A, QB), BF16)
    for s in range(N_HEADS_A // 2):
        t = qa_ref[0, :, s * LANES:(s + 1) * LANES].astype(F32).T.astype(BF16)
        n = (2 * s) // N_GROUP_A
        for c in range(2):
            cols = slice((2 * s + c) * LANES, (2 * s + c + 1) * LANES)
            qt_ref[n * HD_A:(n + 1) * HD_A, cols] = t[c * HD_A:(c + 1) * HD_A, :]
            qt_ref[(1 - n) * HD_A:(2 - n) * HD_A, cols] = zeros_half
    for s in range(N_IDX // 2):
        t = qi_ref[0, :, s * LANES:(s + 1) * LANES].astype(F32).T.astype(BF16)
        for c in range(2):
            cols = slice((2 * s + c) * LANES, (2 * s + c + 1) * LANES)
            qit_ref[:IDX_DIM, cols] = t[c * IDX_DIM:(c + 1) * IDX_DIM, :]
            qit_ref[IDX_DIM:, cols] = zeros_half
    wt = wq_ref[0].T
    w_idx = [wt[IDX_DIM + h:IDX_DIM + h + 1, :] * IDX_SCALE for h in range(N_IDX)]

    n_blk = n_sel * (SEL_BLK // ATT_BLK)
    vregs = ATT_BLK // 8

    def score_dots(blk, slot):
        kblk = kiw_ref[0, pl.ds(pl.multiple_of(blk * ATT_BLK, ATT_BLK), ATT_BLK), :]
        for m in range(N_IDX // 2):
            x = jnp.dot(kblk, qit_ref[:, 2 * m * LANES:(2 * m + 2) * LANES], preferred_element_type=F32)
            x_ref[slot, 2 * m] = x[:, :LANES]
            x_ref[slot, 2 * m + 1] = x[:, LANES:]

    def score_keys(blk, slot):
        off = pl.multiple_of(blk * ATT_BLK, ATT_BLK)
        sc = None
        for h in range(N_IDX):
            term = jnp.maximum(x_ref[slot, h], 0.0) * w_idx[h]
            sc = term if sc is None else sc + term
        bits = pltpu.bitcast(sc, I32)
        key = bits ^ (lax.shift_right_arithmetic(bits, 31) & 0x7FFFFFFF)
        kpos = off + lax.broadcasted_iota(I32, (ATT_BLK, QB), 0)
        key = jnp.where(kpos < lim, key, INT_MIN)
        keys_ref[pl.ds(off, ATT_BLK), :] = key
        a = [key[8 * r:8 * r + 8, :] ^ INT_MIN for r in range(vregs)]
        for d, msk in ((16, 0x0000FFFF), (8, 0x00FF00FF), (4, 0x0F0F0F0F), (2, 0x33333333), (1, 0x55555555)):
            for k in range(vregs):
                if k & d == 0:
                    t = (a[k] ^ lax.shift_right_logical(a[k + d], d)) & msk
                    a[k] = a[k] ^ t
                    a[k + d] = a[k + d] ^ lax.shift_left(t, d)
        for r in range(vregs):
            planes_ref[r, blk] = a[r]

    x_ref[1] = jnp.zeros(x_ref.shape[1:], F32)

    def score_two_blocks(i, carry):
        score_dots(2 * i, 0)
        score_keys(jnp.maximum(2 * i - 1, 0), 1)
        score_dots(2 * i + 1, 1)
        score_keys(2 * i, 0)
        return carry

    lax.fori_loop(0, n_sel, score_two_blocks, 0)
    score_keys(n_blk - 1, 1)

    alive_ref[...] = jnp.full(alive_ref.shape, -1, I32)

    n_grp = lax.div(n_blk + (SEL_GRP - 1), SEL_GRP)

    def zero_blk(i, carry):
        for r in range(vregs):
            planes_ref[r, i] = jnp.zeros((8, QB), I32)
        return carry

    lax.fori_loop(n_blk, n_grp * SEL_GRP, zero_blk, 0)

    def count_ones(p_cur, p_prev, flip):
        def grp(i, acc):
            blks = pl.ds(pl.multiple_of(i * SEL_GRP, SEL_GRP), SEL_GRP)
            alive = alive_ref[blks]
            if p_prev is not None:
                alive = alive & (planes_ref[p_prev, blks] ^ flip[None])
                alive_ref[blks] = alive
            return acc + _fold(jnp.add, lax.population_count(alive & planes_ref[p_cur, blks]))

        acc = lax.fori_loop(0, n_grp, grp, jnp.zeros((8, QB), I32))
        return jnp.sum(acc, axis=0, keepdims=True)

    def decide(p, ones, st):
        thr_u, rem, _ = st
        take = ones >= rem
        bit = lax.shift_left(jnp.int32(1), 31 - p)
        return (jnp.where(take, thr_u | bit, thr_u), jnp.where(take, rem, rem - ones),
                jnp.where(take, 0, -1))

    zero = jnp.zeros((1, QB), I32)
    st = decide(0, count_ones(0, None, None), (zero, zero + topk, zero))
    st = lax.fori_loop(1, 32, lambda p, s: decide(p, count_ones(p, p - 1, s[2]), s), st)
    thr_u, need, flip = st

    def eq_grp(i, acc):
        blks = pl.ds(pl.multiple_of(i * SEL_GRP, SEL_GRP), SEL_GRP)
        return acc + _fold(jnp.add, lax.population_count(alive_ref[blks] & (planes_ref[31, blks] ^ flip[None])))

    n_eq = jnp.sum(lax.fori_loop(0, n_grp, eq_grp, jnp.zeros((8, QB), I32)), axis=0, keepdims=True)
    thr = jnp.maximum(thr_u ^ INT_MIN, INT_MIN + 1)
    tie_cut = jnp.max(jnp.where((n_eq > need) & (thr_u != 0), 1, 0)) > 0

    @pl.when(jnp.logical_not(tie_cut))
    def _():
        def blk(i, carry):
            off = pl.multiple_of(i * ATT_BLK, ATT_BLK)
            key = keys_ref[pl.ds(off, ATT_BLK), :]
            bias_ref[pl.ds(off, ATT_BLK), :] = jnp.where(key >= thr, 0.0, NEG).astype(BF16)
            return carry

        lax.fori_loop(0, n_att, blk, 0)

    @pl.when(tie_cut)
    def _():
        need_f = need.astype(F32)

        def blk(i, run):
            off = pl.multiple_of(i * ATT_BLK, ATT_BLK)
            key = keys_ref[pl.ds(off, ATT_BLK), :]
            eq = key == thr
            pre = jnp.dot(tri_ref[...], jnp.where(eq, 1.0, 0.0).astype(BF16), preferred_element_type=F32)
            keep_eq = (pre + run) <= need_f
            bias = jnp.where(key > thr, 0.0, jnp.where(eq, jnp.where(keep_eq, 0.0, NEG), NEG))
            bias_ref[pl.ds(off, ATT_BLK), :] = bias.astype(BF16)
            return run + pre[ATT_BLK - 1:ATT_BLK, :]

        lax.fori_loop(0, n_att, blk, jnp.zeros((1, QB), F32))

    m_ref[...] = jnp.full(m_ref.shape, NEG, F32)
    acc_ref[...] = jnp.zeros(acc_ref.shape, F32)

    def stage_logits(off, slot):
        kblk = kk_ref[0, pl.ds(off, ATT_BLK), :]
        bias = bias_ref[pl.ds(off, ATT_BLK), :]
        bias2 = jnp.concatenate([bias, bias], axis=1)
        parts = []
        for m in range(npair):
            s = jnp.dot(kblk, qt_ref[:, 2 * m * LANES:(2 * m + 2) * LANES], preferred_element_type=F32)
            sb = s.astype(BF16) + bias2
            sb_ref[slot, m] = sb
            parts.append(_fold(jnp.maximum, sb.reshape(ATT_BLK // PACK16, PACK16, 2 * QB)))
        return parts

    def stage_probs(parts, slot):
        alphas = []
        for m in range(npair):
            m_prev = m_ref[m]
            m_new = jnp.maximum(m_prev, jnp.max(parts[m].astype(F32), axis=0, keepdims=True))
            alphas.append(jnp.exp2(m_prev - m_new))
            sb_ref[slot, m] = jnp.exp2(sb_ref[slot, m] - m_new.astype(BF16))
            m_ref[m] = m_new
        return alphas

    def stage_values(off, slot, alphas):
        for m in range(npair):
            n = (2 * m) // N_GROUP_A
            pv = jnp.dot(vt_ref[0, n * VROWS:(n + 1) * VROWS, pl.ds(off, ATT_BLK)], sb_ref[slot, m],
                         preferred_element_type=F32)
            acc_ref[m] = alphas[m] * acc_ref[m] + pv

    sb_ref[1] = jnp.full(sb_ref.shape[1:], -jnp.inf, BF16)

    def attn_two_blocks(i, parts_prev):
        off0 = pl.multiple_of(i * SEL_BLK, SEL_BLK)
        off1 = pl.multiple_of(off0 + ATT_BLK, ATT_BLK)
        off_prev = pl.multiple_of(jnp.maximum(off0 - ATT_BLK, 0), ATT_BLK)
        parts0 = stage_logits(off0, 0)
        alphas_prev = stage_probs(parts_prev, 1)
        stage_values(off_prev, 1, alphas_prev)
        alphas0 = stage_probs(parts0, 0)
        parts1 = stage_logits(off1, 1)
        stage_values(off0, 0, alphas0)
        return tuple(parts1)

    neg_parts = tuple(jnp.full((PACK16, 2 * QB), NEG, BF16) for _ in range(npair))
    parts_last = lax.fori_loop(0, n_sel, attn_two_blocks, neg_parts)
    alphas_last = stage_probs(parts_last, 1)
    stage_values(pl.multiple_of(n_sel * SEL_BLK - ATT_BLK, ATT_BLK), 1, alphas_last)

    heads = []
    for m in range(npair):
        a = acc_ref[m]
        o = a[:HD_A, :] / a[HD_A:HD_A + 1, :]
        heads += [o[:, :QB], o[:, QB:]]
    o_ref[0] = jnp.concatenate(heads, axis=0).T


def _attention(qa, qi, kiw_q, kk, kiw_k, vt, pos_off, s_real):
    b, t, _ = qa.shape
    s_pad = kk.shape[1]
    topk = min(MAX_TOPK, s_real // 4)
    idx = jnp.arange(ATT_BLK)
    tri = (idx[None, :] <= idx[:, None]).astype(BF16)
    kern = functools.partial(_attn_kernel, pos_off=pos_off, topk=topk, s_real=s_real)
    npair = N_HEADS_A // 2
    nblk_pad = -(-(s_pad // ATT_BLK) // SEL_GRP) * SEL_GRP
    return pl.pallas_call(
        kern,
        grid=(b, t // QB),
        in_specs=[
            pl.BlockSpec((1, QB, W_A), lambda i, j: (i, j, 0)),
            pl.BlockSpec((1, QB, N_IDX * IDX_DIM), lambda i, j: (i, j, 0)),
            pl.BlockSpec((1, QB, LANES), lambda i, j: (i, j, 0)),
            pl.BlockSpec((1, s_pad, LANES), lambda i, j: (i, 0, 0)),
            pl.BlockSpec((1, s_pad, LANES), lambda i, j: (i, 0, 0)),
            pl.BlockSpec((1, N_KV_A * VROWS, s_pad), lambda i, j: (i, 0, 0)),
            pl.BlockSpec((ATT_BLK, ATT_BLK), lambda i, j: (0, 0)),
        ],
        out_specs=pl.BlockSpec((1, QB, W_A), lambda i, j: (i, j, 0)),
        out_shape=jax.ShapeDtypeStruct((b, t, W_A), F32),
        scratch_shapes=[
            pltpu.VMEM((s_pad, QB), I32),
            pltpu.VMEM((ATT_BLK // 8, nblk_pad, 8, QB), I32),
            pltpu.VMEM((nblk_pad, 8, QB), I32),
            pltpu.VMEM((s_pad, QB), BF16),
            pltpu.VMEM((LANES, N_HEADS_A * LANES), BF16),
            pltpu.VMEM((LANES, N_IDX * LANES), BF16),
            pltpu.VMEM((npair, 1, 2 * QB), F32),
            pltpu.VMEM((npair, VROWS, 2 * QB), F32),
            pltpu.VMEM((2, npair, ATT_BLK, 2 * QB), BF16),
            pltpu.VMEM((2, N_IDX, ATT_BLK, QB), F32),
        ],
        compiler_params=pltpu.CompilerParams(dimension_semantics=("arbitrary", "arbitrary"),
                                             vmem_limit_bytes=VMEM_LIMIT),
        name="dsa_attention",
    )(qa, qi, kiw_q, kk, kiw_k, vt, tri)


def _attn_operands(ka, va, kiw, s_pad):
    b, s, _ = ka.shape
    pad = ((0, 0), (0, s_pad - s), (0, 0))
    kk = jnp.pad(ka.astype(BF16), pad)
    kiw_k = jnp.pad(kiw.astype(BF16), pad)
    vb = jnp.pad(va.astype(BF16), pad).reshape(b, s_pad, N_KV_A, HD_A)
    ones = jnp.ones((b, s_pad, N_KV_A, VROWS - HD_A), BF16)
    vaug = jnp.concatenate([vb, ones], axis=-1).reshape(b, s_pad, N_KV_A * VROWS)
    return kk, kiw_k, jnp.swapaxes(vaug, 1, 2)


def _pad_rows(a, t_pad):
    return jnp.pad(a, ((0, 0), (0, t_pad - a.shape[1]), (0, 0)))


def _ret_kernel(q_ref, k_ref, v_ref, s0_ref, dm_ref, qd_ref, kd_ref, sd_ref, o_ref, sn_ref, s_scr,
                *, rt, rc):
    t = pl.program_id(1)

    @pl.when(t == 0)
    def _():
        s_scr[...] = s0_ref[0]

    for c in range(rt // rc):
        rows = slice(c * rc, (c + 1) * rc)
        for h in range(N_HEADS_B):
            cols = slice(h * HD_B, (h + 1) * HD_B)
            q = q_ref[0, rows, cols]
            k = k_ref[0, rows, cols]
            v = v_ref[0, rows, cols]
            att = lax.dot_general(q, k, (((1,), (1,)), ((), ())), preferred_element_type=F32) * dm_ref[h]
            s = s_scr[h]
            o = (jnp.dot(att.astype(BF16), v, preferred_element_type=F32)
                 + jnp.dot(q, s.astype(BF16), preferred_element_type=F32) * qd_ref[h])
            kd = (k.astype(F32) * kd_ref[h]).astype(BF16)
            s_scr[h] = s * sd_ref[h] + lax.dot_general(kd, v, (((0,), (0,)), ((), ())),
                                                       preferred_element_type=F32)
            mu = jnp.mean(o, axis=-1, keepdims=True)
            d = o - mu
            var = jnp.mean(d * d, axis=-1, keepdims=True)
            o_ref[0, rows, cols] = d * lax.rsqrt(var + EPS)

    @pl.when(t == pl.num_programs(1) - 1)
    def _():
        sn_ref[0] = s_scr[...]


def _retention(qb_, kb_, vb_, s0):
    b, t, _ = qb_.shape
    rt = min(512, t)
    rc = min(128, t)
    lg = jnp.log1p(-jnp.exp2(-5.0 - jnp.arange(N_HEADS_B, dtype=F32)))
    i = jnp.arange(rc, dtype=F32)
    diff = i[:, None] - i[None, :]
    dmask = jnp.exp(jnp.where(diff >= 0, diff[None] * lg[:, None, None], -jnp.inf))
    ones = jnp.ones((1, 1, HD_B), F32)
    q_dec = jnp.exp((i + 1.0)[None, :] * lg[:, None])[:, :, None] * ones
    k_dec = jnp.exp((rc - 1.0 - i)[None, :] * lg[:, None])[:, :, None] * ones
    s_dec = jnp.exp(rc * lg)[:, None, None] * jnp.ones((1, HD_B, HD_B), F32)
    blk = pl.BlockSpec((1, rt, W_B), lambda i_, j: (i_, j, 0))
    st = pl.BlockSpec((1, N_HEADS_B, HD_B, HD_B), lambda i_, j: (i_, 0, 0, 0))
    c3 = lambda i_, j: (0, 0, 0)
    kern = functools.partial(_ret_kernel, rt=rt, rc=rc)
    return pl.pallas_call(
        kern,
        grid=(b, t // rt),
        in_specs=[blk, blk, blk, st,
                  pl.BlockSpec((N_HEADS_B, rc, rc), c3),
                  pl.BlockSpec((N_HEADS_B, rc, HD_B), c3),
                  pl.BlockSpec((N_HEADS_B, rc, HD_B), c3),
                  pl.BlockSpec((N_HEADS_B, HD_B, HD_B), c3)],
        out_specs=[blk, st],
        out_shape=[jax.ShapeDtypeStruct((b, t, W_B), F32),
                   jax.ShapeDtypeStruct((b, N_HEADS_B, HD_B, HD_B), F32)],
        scratch_shapes=[pltpu.VMEM((N_HEADS_B, HD_B, HD_B), F32)],
        compiler_params=pltpu.CompilerParams(dimension_semantics=("arbitrary", "arbitrary"),
                                             vmem_limit_bytes=VMEM_LIMIT),
        name="retention",
    )(qb_, kb_, vb_, s0, dmask, q_dec, k_dec, s_dec)


def _outproj_kernel(x_ref, oa_ref, ga_ref, ob_ref, gb_ref, w_ref, g_ref, gate_ref, o_ref):
    ma = (oa_ref[0] * ga_ref[0].astype(F32)).astype(BF16)
    mb = (ob_ref[0] * gb_ref[0].astype(F32)).astype(BF16)
    y = (jnp.dot(ma, w_ref[:W_A, :], preferred_element_type=F32)
         + jnp.dot(mb, w_ref[W_A:, :], preferred_element_type=F32))
    ms = jnp.mean(y * y, axis=-1, keepdims=True)
    yn = y * lax.rsqrt(ms + EPS) * g_ref[...]
    o_ref[0] = x_ref[0] + gate_ref[0] * yn


def _outproj(x, oa, ga, ob, gb, w_out, g_post, gate):
    b, t, _ = x.shape
    tm = _row_tile(t)
    row = lambda w: pl.BlockSpec((1, tm, w), lambda i, j: (i, j, 0))
    return pl.pallas_call(
        _outproj_kernel,
        grid=(b, t // tm),
        in_specs=[row(D_MODEL), row(W_A), row(W_A), row(W_B), row(W_B),
                  pl.BlockSpec((D_MODEL, D_MODEL), lambda i, j: (0, 0)),
                  pl.BlockSpec((1, D_MODEL), lambda i, j: (0, 0)),
                  pl.BlockSpec((1, 1, D_MODEL), lambda i, j: (i, 0, 0))],
        out_specs=row(D_MODEL),
        out_shape=jax.ShapeDtypeStruct((b, t, D_MODEL), F32),
        compiler_params=pltpu.CompilerParams(dimension_semantics=("arbitrary", "arbitrary"),
                                             vmem_limit_bytes=VMEM_LIMIT),
        name="outproj",
    )(x, oa, ga, ob, gb, w_out.astype(BF16), g_post.reshape(1, D_MODEL), gate)


def _layer(x, mod, pos_off, tabs, wp, g_pre, w_out, g_post, past):
    b, t, _ = x.shape
    shift, scale, gate = jnp.split(mod, 3, axis=-1)
    p = _inproj(x, (1.0 + scale)[:, None, :], shift[:, None, :], g_pre, wp, tabs)
    ka, va, kiw = p["ka"], p["va"], p["kiw"]
    if past is None:
        k_all, v_all, kiw_all = ka, va, kiw
        s0 = jnp.zeros((b, N_HEADS_B, HD_B, HD_B), F32)
    else:
        pk, pv, pki, s0 = past
        plen = pk.shape[1]
        k_all = jnp.concatenate([pk.reshape(b, plen, N_KV_A * HD_A), ka], axis=1)
        v_all = jnp.concatenate([pv.reshape(b, plen, N_KV_A * HD_A), va], axis=1)
        pki = jnp.pad(pki, ((0, 0), (0, 0), (0, LANES - IDX_DIM)))
        kiw_all = jnp.concatenate([pki, kiw], axis=1)
    s_real = k_all.shape[1]
    s_pad = -(-s_real // SEL_BLK) * SEL_BLK
    t_pad = -(-t // QB) * QB
    kk, kiw_k, vt = _attn_operands(k_all, v_all, kiw_all, s_pad)
    oa = _attention(_pad_rows(p["qa"], t_pad), _pad_rows(p["qi"], t_pad), _pad_rows(kiw, t_pad),
                    kk, kiw_k, vt, pos_off, s_real)[:, :t]
    ob, s_new = _retention(p["qb"], p["kb"], p["vb"], s0)
    x_new = _outproj(x, oa, p["ga"], ob, p["gb"], w_out, g_post, gate[:, None, :])
    k_n = ka.reshape(b, t, N_KV_A, HD_A)
    v_n = va.reshape(b, t, N_KV_A, HD_A)
    return x_new, (k_n, v_n, kiw[..., :IDX_DIM], s_new)


def kernel(x_prompt, x_sample, cache_k_a, cache_v_a, cache_kidx_a, state_s_b, c_prompt, c_sample,
           w_ada, b_ada, g_pre, w_in, w_out, g_post):
    depth = w_in.shape[0]
    nb = x_prompt.shape[0]
    past_len = cache_k_a.shape[2]
    tabs_p = _rope_tables(jnp.arange(x_prompt.shape[1], dtype=jnp.int32))
    tabs_s = _rope_tables(past_len + jnp.arange(x_sample.shape[1], dtype=jnp.int32))
    mod = _modulation(jnp.concatenate([c_prompt, c_sample], axis=0), w_ada, b_ada)
    hp, hs = x_prompt, x_sample
    outs_p, outs_s = [], []
    for l in range(depth):
        wp = _prep_w_in(w_in[l])
        hp, new_p = _layer(hp, mod[l, :nb], 0, tabs_p, wp, g_pre[l], w_out[l], g_post[l], None)
        outs_p.append(new_p)
        hs, new_s = _layer(hs, mod[l, nb:], past_len, tabs_s, wp, g_pre[l], w_out[l], g_post[l],
                           (cache_k_a[l], cache_v_a[l], cache_kidx_a[l], state_s_b[l]))
        outs_s.append(new_s)
    stack = lambda outs, i: jnp.stack([o[i] for o in outs])
    return (hp, hs,
            stack(outs_p, 0), stack(outs_p, 1), stack(outs_p, 2), stack(outs_p, 3),
            stack(outs_s, 0), stack(outs_s, 1), stack(outs_s, 2), stack(outs_s, 3))
```

```python
import functools
import math

import jax
import jax.numpy as jnp
from jax import lax
from jax.experimental import pallas as pl
from jax.experimental.pallas import tpu as pltpu

D_MODEL = 1024
CHUNK = 64
W_A = 512
HD_A = 64
N_HEADS_A = 8
N_KV_A = 2
N_GROUP_A = N_HEADS_A // N_KV_A
N_IDX = 4
IDX_DIM = 64
MAX_TOPK = 256
HD_B = 128
N_HEADS_B = 4
W_B = 512
ROPE_THETA = 10000.0
EPS = 1e-6
IDX_SCALE = (IDX_DIM ** -0.5) * (N_IDX ** -0.5)
LOG2E = math.log2(math.e)

LANES = 128
PACK16 = 16
VMEM_LIMIT = 56 * 1024 * 1024
INT_MIN = -(2 ** 31)
NEG = -1e30

QB = LANES
SEL_BLK = 512
ATT_BLK = 256
SEL_GRP = 8
VROWS = HD_A + PACK16

F32 = jnp.float32
BF16 = jnp.bfloat16
I32 = jnp.int32

_SEGS = (
    ("qa", W_A, "a"),
    ("ka", N_KV_A * HD_A, "a"),
    ("va", N_KV_A * HD_A, None),
    ("ga", W_A, None),
    ("qi", N_IDX * IDX_DIM, "a"),
    ("kiw", LANES, "k"),
    ("qb", W_B, "b"),
    ("kb", W_B, "b"),
    ("vb", W_B, None),
    ("gb", W_B, None),
)
_W_COLS = sum(s[1] for s in _SEGS)
_OUT_DTYPES = {"qa": BF16, "ka": F32, "va": F32, "ga": BF16, "qi": BF16, "kiw": F32,
               "qb": BF16, "kb": BF16, "vb": BF16, "gb": BF16}


def _silu(x):
    return x / (1.0 + jnp.exp(-x))


def _fold(op, x):
    parts = [x[i] for i in range(x.shape[0])]
    while len(parts) > 1:
        parts = [op(parts[i], parts[i + 1]) for i in range(0, len(parts) - 1, 2)] + parts[len(parts) & ~1:]
    return parts[0]


def _mod_kernel(c_ref, w_ref, b_ref, o_ref):
    c = c_ref[...]
    s = _silu(c)
    o_ref[0] = jnp.dot(s.astype(BF16), w_ref[0].astype(BF16), preferred_element_type=F32) + b_ref[0]


def _modulation(c_all, w_ada, b_ada):
    depth = w_ada.shape[0]
    n = c_all.shape[0]
    nblk = 3
    return pl.pallas_call(
        _mod_kernel,
        grid=(depth, nblk),
        in_specs=[
            pl.BlockSpec((n, D_MODEL), lambda l, j: (0, 0)),
            pl.BlockSpec((1, D_MODEL, D_MODEL), lambda l, j: (l, 0, j)),
            pl.BlockSpec((1, 1, D_MODEL), lambda l, j: (l, 0, j)),
        ],
        out_specs=pl.BlockSpec((1, n, D_MODEL), lambda l, j: (l, 0, j)),
        out_shape=jax.ShapeDtypeStruct((depth, n, 3 * D_MODEL), F32),
        compiler_params=pltpu.CompilerParams(dimension_semantics=("arbitrary", "arbitrary"),
                                             vmem_limit_bytes=VMEM_LIMIT),
        name="modulation",
    )(c_all, w_ada, b_ada.reshape(depth, 1, 3 * D_MODEL))


def _rope_slab(p, kind, tabs):
    if kind == "b":
        cos, sin_s = tabs["b"]
        return p * cos[...] + pltpu.roll(p, HD_B // 2, axis=1) * sin_s[...]
    cos, sin_lo, sin_hi = tabs[kind]
    return (p * cos[...] + pltpu.roll(p, LANES - HD_A // 2, axis=1) * sin_lo[...]
            + pltpu.roll(p, HD_A // 2, axis=1) * sin_hi[...])


def _inproj_kernel(x_ref, sc_ref, sh_ref, g_ref, w_ref,
                   ca_ref, sal_ref, sah_ref, ck_ref, skl_ref, skh_ref, cb_ref, sbs_ref, *out_refs):
    x = x_ref[0]
    ms = jnp.mean(x * x, axis=-1, keepdims=True)
    y = x * lax.rsqrt(ms + EPS) * g_ref[...]
    h = y * sc_ref[0] + sh_ref[0]
    hb = h.astype(BF16)
    tabs = {"a": (ca_ref, sal_ref, sah_ref), "k": (ck_ref, skl_ref, skh_ref), "b": (cb_ref, sbs_ref)}
    c0 = 0
    for (name, width, kind), o_ref in zip(_SEGS, out_refs):
        step = 2 * LANES if width % (2 * LANES) == 0 else LANES
        for off in range(0, width, step):
            p = jnp.dot(hb, w_ref[:, c0 + off:c0 + off + step], preferred_element_type=F32)
            for s in range(step // LANES):
                ps = p[:, s * LANES:(s + 1) * LANES]
                if kind is not None:
                    ps = _rope_slab(ps, kind, tabs)
                if name == "kb":
                    ps = ps * (HD_B ** -0.5)
                if name in ("ga", "gb"):
                    ps = _silu(ps)
                o_ref[0, :, off + s * LANES:off + (s + 1) * LANES] = ps.astype(o_ref.dtype)
        c0 += width


def _prep_w_in(w):
    sizes = (W_A, N_KV_A * HD_A, N_KV_A * HD_A, W_A, N_IDX * IDX_DIM, IDX_DIM, N_IDX, W_B, W_B, W_B, W_B)
    parts = []
    o = 0
    for n in sizes:
        parts.append(w[:, o:o + n])
        o += n
    q_a, k_a, v_a, g_a, q_i, k_i, w_i, q_b, k_b, v_b, g_b = parts
    kk = w.shape[0]
    q_a = q_a * ((HD_A ** -0.5) * LOG2E)
    kiw = jnp.concatenate([k_i, w_i, jnp.zeros((kk, LANES - IDX_DIM - N_IDX), w.dtype)], axis=1)
    segs = {"qa": q_a, "ka": k_a, "va": v_a, "ga": g_a, "qi": q_i, "kiw": kiw,
            "qb": q_b, "kb": k_b, "vb": v_b, "gb": g_b}
    return jnp.concatenate([segs[s[0]] for s in _SEGS], axis=1).astype(BF16)


def _rope_tables(pos):
    posf = pos.astype(F32)[:, None]
    lane = jnp.arange(LANES)[None, :]

    def tab(d):
        half = d // 2
        inv = ROPE_THETA ** (-jnp.arange(half, dtype=F32) * 2.0 / d)
        ang = posf * inv[None, :]
        reps = LANES // half
        return jnp.tile(jnp.cos(ang), (1, reps)), jnp.tile(jnp.sin(ang), (1, reps)), (lane % d) < half

    ca, sa, lo_a = tab(HD_A)
    sal, sah = jnp.where(lo_a, -sa, 0.0), jnp.where(lo_a, 0.0, sa)
    in_k = lane < IDX_DIM
    ck, skl, skh = jnp.where(in_k, ca, 1.0), jnp.where(in_k, sal, 0.0), jnp.where(in_k, sah, 0.0)
    cb, sb, lo_b = tab(HD_B)
    return ca, sal, sah, ck, skl, skh, cb, jnp.where(lo_b, -sb, sb)


def _row_tile(t):
    return min(512, t)


def _inproj(x, scale1p, shift, g_pre, wp, tabs):
    b, t, _ = x.shape
    tm = _row_tile(t)
    nt = t // tm
    const = lambda i, j: (0, 0)
    tab_spec = pl.BlockSpec((tm, LANES), lambda i, j: (j, 0))
    in_specs = [
        pl.BlockSpec((1, tm, D_MODEL), lambda i, j: (i, j, 0)),
        pl.BlockSpec((1, 1, D_MODEL), lambda i, j: (i, 0, 0)),
        pl.BlockSpec((1, 1, D_MODEL), lambda i, j: (i, 0, 0)),
        pl.BlockSpec((1, D_MODEL), const),
        pl.BlockSpec((D_MODEL, _W_COLS), const, pipeline_mode=pl.Buffered(1)),
    ] + [tab_spec] * len(tabs)
    out_specs = [pl.BlockSpec((1, tm, s[1]), lambda i, j: (i, j, 0)) for s in _SEGS]
    out_shape = [jax.ShapeDtypeStruct((b, t, s[1]), _OUT_DTYPES[s[0]]) for s in _SEGS]
    outs = pl.pallas_call(
        _inproj_kernel,
        grid=(b, nt),
        in_specs=in_specs,
        out_specs=out_specs,
        out_shape=out_shape,
        compiler_params=pltpu.CompilerParams(dimension_semantics=("arbitrary", "arbitrary"),
                                             vmem_limit_bytes=VMEM_LIMIT),
        name="inproj",
    )(x, scale1p, shift, g_pre.reshape(1, D_MODEL), wp, *tabs)
    return dict(zip([s[0] for s in _SEGS], outs))


def _attn_kernel(qa_ref, qi_ref, wq_ref, kk_ref, kiw_ref, vt_ref, tri_ref, o_ref,
                 keys_ref, planes_ref, alive_ref, qt_ref, qit_ref, m_ref, acc_ref, sb_ref, x_ref,
                 *, pos_off, topk, s_real):
    j = pl.program_id(1)
    npair = N_HEADS_A // 2

    qlane = lax.broadcasted_iota(I32, (1, QB), 1)
    pos = pos_off + j * QB + qlane
    lim = jnp.minimum((lax.shift_right_logical(pos, 6) + 1) * CHUNK, s_real)
    last_pos = pos_off + (j + 1) * QB - 1
    lim_max = jnp.minimum((lax.shift_right_logical(last_pos, 6) + 1) * CHUNK, s_real)
    n_sel = lax.div(lim_max + (SEL_BLK - 1), SEL_BLK)
    n_att = n_sel * (SEL_BLK // ATT_BLK)

    zeros_half = jnp.zeros((HD_A, QB), BF16)
    for s in range(N_HEADS_A // 2):
        t = qa_ref[0, :, s * LANES:(s + 1) * LANES].astype(F32).T.astype(BF16)
        n = (2 * s) // N_GROUP_A
        for c in range(2):
            cols = slice((2 * s + c) * LANES, (2 * s + c + 1) * LANES)
            qt_ref[n * HD_A:(n + 1) * HD_A, cols] = t[c * HD_A:(c + 1) * HD_A, :]
            qt_ref[(1 - n) * HD_A:(2 - n) * HD_A, cols] = zeros_half
    for s in range(N_IDX // 2):
        t = qi_ref[0, :, s * LANES:(s + 1) * LANES].astype(F32).T.astype(BF16)
        for c in range(2):
            cols = slice((2 * s + c) * LANES, (2 * s + c + 1) * LANES)
            qit_ref[:IDX_DIM, cols] = t[c * IDX_DIM:(c + 1) * IDX_DIM, :]
            qit_ref[IDX_DIM:, cols] = zeros_half
    wt = wq_ref[0].T
    w_idx = [wt[IDX_DIM + h:IDX_DIM + h + 1, :] * IDX_SCALE for h in range(N_IDX)]

    n_blk = n_sel * (SEL_BLK // ATT_BLK)
    vregs = ATT_BLK // 8

    def score_dots(blk, slot):
        kblk = kiw_ref[0, pl.ds(pl.multiple_of(blk * ATT_BLK, ATT_BLK), ATT_BLK), :]
        for m in range(N_IDX // 2):
            x = jnp.dot(kblk, qit_ref[:, 2 * m * LANES:(2 * m + 2) * LANES], preferred_element_type=F32)
            x_ref[slot, 2 * m] = x[:, :LANES]
            x_ref[slot, 2 * m + 1] = x[:, LANES:]

    def score_keys(blk, slot):
        off = pl.multiple_of(blk * ATT_BLK, ATT_BLK)
        sc = None
        for h in range(N_IDX):
            term = jnp.maximum(x_ref[slot, h], 0.0) * w_idx[h]
            sc = term if sc is None else sc + term
        bits = pltpu.bitcast(sc, I32)
        key = bits ^ (lax.shift_right_arithmetic(bits, 31) & 0x7FFFFFFF)
        kpos = off + lax.broadcasted_iota(I32, (ATT_BLK, QB), 0)
        key = jnp.where(kpos < lim, key, INT_MIN)
        keys_ref[pl.ds(off, ATT_BLK), :] = key
        a = [key[8 * r:8 * r + 8, :] ^ INT_MIN for r in range(vregs)]
        for d, msk in ((16, 0x0000FFFF), (8, 0x00FF00FF), (4, 0x0F0F0F0F), (2, 0x33333333), (1, 0x55555555)):
            for k in range(vregs):
                if k & d == 0:
                    t = (a[k] ^ lax.shift_right_logical(a[k + d], d)) & msk
                    a[k] = a[k] ^ t
                    a[k + d] = a[k + d] ^ lax.shift_left(t, d)
        for r in range(vregs):
            planes_ref[r, blk] = a[r]

    x_ref[1] = jnp.zeros(x_ref.shape[1:], F32)

    def score_two_blocks(i, carry):
        score_dots(2 * i, 0)
        score_keys(jnp.maximum(2 * i - 1, 0), 1)
        score_dots(2 * i + 1, 1)
        score_keys(2 * i, 0)
        return carry

    lax.fori_loop(0, n_sel, score_two_blocks, 0)
    score_keys(n_blk - 1, 1)

    alive_ref[...] = jnp.full(alive_ref.shape, -1, I32)

    n_grp = lax.div(n_blk + (SEL_GRP - 1), SEL_GRP)

    def zero_blk(i, carry):
        for r in range(vregs):
            planes_ref[r, i] = jnp.zeros((8, QB), I32)
        return carry

    lax.fori_loop(n_blk, n_grp * SEL_GRP, zero_blk, 0)

    def count_ones(p_cur, p_prev, flip):
        def grp(i, acc):
            blks = pl.ds(pl.multiple_of(i * SEL_GRP, SEL_GRP), SEL_GRP)
            alive = alive_ref[blks]
            if p_prev is not None:
                alive = alive & (planes_ref[p_prev, blks] ^ flip[None])
                alive_ref[blks] = alive
            return acc + _fold(jnp.add, lax.population_count(alive & planes_ref[p_cur, blks]))

        acc = lax.fori_loop(0, n_grp, grp, jnp.zeros((8, QB), I32))
        return jnp.sum(acc, axis=0, keepdims=True)

    def decide(p, ones, st):
        thr_u, rem, _ = st
        take = ones >= rem
        bit = lax.shift_left(jnp.int32(1), 31 - p)
        return (jnp.where(take, thr_u | bit, thr_u), jnp.where(take, rem, rem - ones),
                jnp.where(take, 0, -1))

    zero = jnp.zeros((1, QB), I32)
    st = decide(0, count_ones(0, None, None), (zero, zero + topk, zero))
    st = lax.fori_loop(1, 32, lambda p, s: decide(p, count_ones(p, p - 1, s[2]), s), st)
    thr_u, need, flip = st

    def eq_grp(i, acc):
        blks = pl.ds(pl.multiple_of(i * SEL_GRP, SEL_GRP), SEL_GRP)
        return acc + _fold(jnp.add, lax.population_count(alive_ref[blks] & (planes_ref[31, blks] ^ flip[None])))

    n_eq = jnp.sum(lax.fori_loop(0, n_grp, eq_grp, jnp.zeros((8, QB), I32)), axis=0, keepdims=True)
    thr = jnp.maximum(thr_u ^ INT_MIN, INT_MIN + 1)
    tie_cut = jnp.max(jnp.where((n_eq > need) & (thr_u != 0), 1, 0)) > 0

    @pl.when(tie_cut)
    def _():
        need_f = need.astype(F32)

        def blk(i, run):
            off = pl.multiple_of(i * ATT_BLK, ATT_BLK)
            key = keys_ref[pl.ds(off, ATT_BLK), :]
            eq = key == thr
            pre = jnp.dot(tri_ref[...], jnp.where(eq, 1.0, 0.0).astype(BF16), preferred_element_type=F32)
            keep_eq = (pre + run) <= need_f
            keys_ref[pl.ds(off, ATT_BLK), :] = jnp.where(eq, jnp.where(keep_eq, key, INT_MIN), key)
            return run + pre[ATT_BLK - 1:ATT_BLK, :]

        lax.fori_loop(0, n_att, blk, jnp.zeros((1, QB), F32))

    m_ref[...] = jnp.full(m_ref.shape, NEG, F32)
    acc_ref[...] = jnp.zeros(acc_ref.shape, F32)

    def stage_logits(off, slot):
        kblk = kk_ref[0, pl.ds(off, ATT_BLK), :]
        bias = jnp.where(keys_ref[pl.ds(off, ATT_BLK), :] >= thr, 0.0, NEG).astype(BF16)
        bias2 = jnp.concatenate([bias, bias], axis=1)
        parts = []
        for m in range(npair):
            s = jnp.dot(kblk, qt_ref[:, 2 * m * LANES:(2 * m + 2) * LANES], preferred_element_type=F32)
            sb = s.astype(BF16) + bias2
            sb_ref[slot, m] = sb
            parts.append(_fold(jnp.maximum, sb.reshape(ATT_BLK // PACK16, PACK16, 2 * QB)))
        return parts

    def stage_probs(parts, slot):
        alphas = []
        for m in range(npair):
            m_prev = m_ref[m]
            m_new = jnp.maximum(m_prev, jnp.max(parts[m].astype(F32), axis=0, keepdims=True))
            alphas.append(jnp.exp2(m_prev - m_new))
            sb_ref[slot, m] = jnp.exp2(sb_ref[slot, m] - m_new.astype(BF16))
            m_ref[m] = m_new
        return alphas

    def stage_values(off, slot, alphas):
        for m in range(npair):
            n = (2 * m) // N_GROUP_A
            pv = jnp.dot(vt_ref[0, n * VROWS:(n + 1) * VROWS, pl.ds(off, ATT_BLK)], sb_ref[slot, m],
                         preferred_element_type=F32)
            acc_ref[m] = alphas[m] * acc_ref[m] + pv

    sb_ref[1] = jnp.full(sb_ref.shape[1:], -jnp.inf, BF16)

    def attn_two_blocks(i, parts_prev):
        off0 = pl.multiple_of(i * SEL_BLK, SEL_BLK)
        off1 = pl.multiple_of(off0 + ATT_BLK, ATT_BLK)
        off_prev = pl.multiple_of(jnp.maximum(off0 - ATT_BLK, 0), ATT_BLK)
        parts0 = stage_logits(off0, 0)
        alphas_prev = stage_probs(parts_prev, 1)
        stage_values(off_prev, 1, alphas_prev)
        alphas0 = stage_probs(parts0, 0)
        parts1 = stage_logits(off1, 1)
        stage_values(off0, 0, alphas0)
        return tuple(parts1)

    neg_parts = tuple(jnp.full((PACK16, 2 * QB), NEG, BF16) for _ in range(npair))
    parts_last = lax.fori_loop(0, n_sel, attn_two_blocks, neg_parts)
    alphas_last = stage_probs(parts_last, 1)
    stage_values(pl.multiple_of(n_sel * SEL_BLK - ATT_BLK, ATT_BLK), 1, alphas_last)

    heads = []
    for m in range(npair):
        a = acc_ref[m]
        o = a[:HD_A, :] / a[HD_A:HD_A + 1, :]
        heads += [o[:, :QB], o[:, QB:]]
    o_ref[0] = jnp.concatenate(heads, axis=0).T.astype(o_ref.dtype)


def _attention(qa, qi, kiw_q, kk, kiw_k, vt, pos_off, s_real):
    b, t, _ = qa.shape
    s_pad = kk.shape[1]
    topk = min(MAX_TOPK, s_real // 4)
    idx = jnp.arange(ATT_BLK)
    tri = (idx[None, :] <= idx[:, None]).astype(BF16)
    kern = functools.partial(_attn_kernel, pos_off=pos_off, topk=topk, s_real=s_real)
    npair = N_HEADS_A // 2
    nblk_pad = -(-(s_pad // ATT_BLK) // SEL_GRP) * SEL_GRP
    return pl.pallas_call(
        kern,
        grid=(b, t // QB),
        in_specs=[
            pl.BlockSpec((1, QB, W_A), lambda i, j: (i, j, 0)),
            pl.BlockSpec((1, QB, N_IDX * IDX_DIM), lambda i, j: (i, j, 0)),
            pl.BlockSpec((1, QB, LANES), lambda i, j: (i, j, 0)),
            pl.BlockSpec((1, s_pad, LANES), lambda i, j: (i, 0, 0)),
            pl.BlockSpec((1, s_pad, LANES), lambda i, j: (i, 0, 0)),
            pl.BlockSpec((1, N_KV_A * VROWS, s_pad), lambda i, j: (i, 0, 0)),
            pl.BlockSpec((ATT_BLK, ATT_BLK), lambda i, j: (0, 0)),
        ],
        out_specs=pl.BlockSpec((1, QB, W_A), lambda i, j: (i, j, 0)),
        out_shape=jax.ShapeDtypeStruct((b, t, W_A), BF16),
        scratch_shapes=[
            pltpu.VMEM((s_pad, QB), I32),
            pltpu.VMEM((ATT_BLK // 8, nblk_pad, 8, QB), I32),
            pltpu.VMEM((nblk_pad, 8, QB), I32),
            pltpu.VMEM((LANES, N_HEADS_A * LANES), BF16),
            pltpu.VMEM((LANES, N_IDX * LANES), BF16),
            pltpu.VMEM((npair, 1, 2 * QB), F32),
            pltpu.VMEM((npair, VROWS, 2 * QB), F32),
            pltpu.VMEM((2, npair, ATT_BLK, 2 * QB), BF16),
            pltpu.VMEM((2, N_IDX, ATT_BLK, QB), F32),
        ],
        compiler_params=pltpu.CompilerParams(dimension_semantics=("arbitrary", "arbitrary"),
                                             vmem_limit_bytes=VMEM_LIMIT),
        name="dsa_attention",
    )(qa, qi, kiw_q, kk, kiw_k, vt, tri)


def _attn_operands(ka, va, kiw, s_pad):
    b, s, _ = ka.shape
    pad = ((0, 0), (0, s_pad - s), (0, 0))
    kk = jnp.pad(ka.astype(BF16), pad)
    kiw_k = jnp.pad(kiw.astype(BF16), pad)
    vb = jnp.pad(va.astype(BF16), pad).reshape(b, s_pad, N_KV_A, HD_A)
    ones = jnp.ones((b, s_pad, N_KV_A, VROWS - HD_A), BF16)
    vaug = jnp.concatenate([vb, ones], axis=-1).reshape(b, s_pad, N_KV_A * VROWS)
    return kk, kiw_k, jnp.swapaxes(vaug, 1, 2)


def _pad_rows(a, t_pad):
    return jnp.pad(a, ((0, 0), (0, t_pad - a.shape[1]), (0, 0)))


def _ret_kernel(q_ref, k_ref, v_ref, s0_ref, dm_ref, qd_ref, kd_ref, sd_ref, o_ref, sn_ref, s_scr,
                *, rt, rc):
    t = pl.program_id(1)

    @pl.when(t == 0)
    def _():
        s_scr[...] = s0_ref[0]

    heads = range(N_HEADS_B)
    nt_dims = (((1,), (1,)), ((), ()))
    tn_dims = (((0,), (0,)), ((), ()))
    for c in range(rt // rc):
        rows = slice(c * rc, (c + 1) * rc)
        q = [q_ref[0, rows, h * HD_B:(h + 1) * HD_B] for h in heads]
        k = [k_ref[0, rows, h * HD_B:(h + 1) * HD_B] for h in heads]
        v = [v_ref[0, rows, h * HD_B:(h + 1) * HD_B] for h in heads]
        s = [s_scr[h] for h in heads]
        att = [lax.dot_general(q[h], k[h], nt_dims, preferred_element_type=F32) * dm_ref[h] for h in heads]
        qs = [jnp.dot(q[h], s[h].astype(BF16), preferred_element_type=F32) * qd_ref[h] for h in heads]
        kd = [(k[h].astype(F32) * kd_ref[h]).astype(BF16) for h in heads]
        kv = [lax.dot_general(kd[h], v[h], tn_dims, preferred_element_type=F32) for h in heads]
        o = [jnp.dot(att[h].astype(BF16), v[h], preferred_element_type=F32) + qs[h] for h in heads]
        for h in heads:
            s_scr[h] = s[h] * sd_ref[h] + kv[h]
            mu = jnp.mean(o[h], axis=-1, keepdims=True)
            d = o[h] - mu
            var = jnp.mean(d * d, axis=-1, keepdims=True)
            o_ref[0, rows, h * HD_B:(h + 1) * HD_B] = (d * lax.rsqrt(var + EPS)).astype(o_ref.dtype)

    @pl.when(t == pl.num_programs(1) - 1)
    def _():
        sn_ref[0] = s_scr[...]


def _retention(qb_, kb_, vb_, s0):
    b, t, _ = qb_.shape
    rt = min(512, t)
    rc = min(128, t)
    lg = jnp.log1p(-jnp.exp2(-5.0 - jnp.arange(N_HEADS_B, dtype=F32)))
    i = jnp.arange(rc, dtype=F32)
    diff = i[:, None] - i[None, :]
    dmask = jnp.exp(jnp.where(diff >= 0, diff[None] * lg[:, None, None], -jnp.inf))
    ones = jnp.ones((1, 1, HD_B), F32)
    q_dec = jnp.exp((i + 1.0)[None, :] * lg[:, None])[:, :, None] * ones
    k_dec = jnp.exp((rc - 1.0 - i)[None, :] * lg[:, None])[:, :, None] * ones
    s_dec = jnp.exp(rc * lg)[:, None, None] * jnp.ones((1, HD_B, HD_B), F32)
    blk = pl.BlockSpec((1, rt, W_B), lambda i_, j: (i_, j, 0))
    st = pl.BlockSpec((1, N_HEADS_B, HD_B, HD_B), lambda i_, j: (i_, 0, 0, 0))
    c3 = lambda i_, j: (0, 0, 0)
    kern = functools.partial(_ret_kernel, rt=rt, rc=rc)
    return pl.pallas_call(
        kern,
        grid=(b, t // rt),
        in_specs=[blk, blk, blk, st,
                  pl.BlockSpec((N_HEADS_B, rc, rc), c3),
                  pl.BlockSpec((N_HEADS_B, rc, HD_B), c3),
                  pl.BlockSpec((N_HEADS_B, rc, HD_B), c3),
                  pl.BlockSpec((N_HEADS_B, HD_B, HD_B), c3)],
        out_specs=[blk, st],
        out_shape=[jax.ShapeDtypeStruct((b, t, W_B), BF16),
                   jax.ShapeDtypeStruct((b, N_HEADS_B, HD_B, HD_B), F32)],
        scratch_shapes=[pltpu.VMEM((N_HEADS_B, HD_B, HD_B), F32)],
        compiler_params=pltpu.CompilerParams(dimension_semantics=("arbitrary", "arbitrary"),
                                             vmem_limit_bytes=VMEM_LIMIT),
        name="retention",
    )(qb_, kb_, vb_, s0, dmask, q_dec, k_dec, s_dec)


def _outproj_kernel(x_ref, oa_ref, ga_ref, ob_ref, gb_ref, w_ref, g_ref, gate_ref, o_ref):
    ma = (oa_ref[0].astype(F32) * ga_ref[0].astype(F32)).astype(BF16)
    mb = (ob_ref[0].astype(F32) * gb_ref[0].astype(F32)).astype(BF16)
    y = (jnp.dot(ma, w_ref[:W_A, :], preferred_element_type=F32)
         + jnp.dot(mb, w_ref[W_A:, :], preferred_element_type=F32))
    ms = jnp.mean(y * y, axis=-1, keepdims=True)
    yn = y * lax.rsqrt(ms + EPS) * g_ref[...]
    o_ref[0] = x_ref[0] + gate_ref[0] * yn


def _outproj(x, oa, ga, ob, gb, w_out, g_post, gate):
    b, t, _ = x.shape
    tm = _row_tile(t)
    row = lambda w: pl.BlockSpec((1, tm, w), lambda i, j: (i, j, 0))
    return pl.pallas_call(
        _outproj_kernel,
        grid=(b, t // tm),
        in_specs=[row(D_MODEL), row(W_A), row(W_A), row(W_B), row(W_B),
                  pl.BlockSpec((D_MODEL, D_MODEL), lambda i, j: (0, 0)),
                  pl.BlockSpec((1, D_MODEL), lambda i, j: (0, 0)),
                  pl.BlockSpec((1, 1, D_MODEL), lambda i, j: (i, 0, 0))],
        out_specs=row(D_MODEL),
        out_shape=jax.ShapeDtypeStruct((b, t, D_MODEL), F32),
        compiler_params=pltpu.CompilerParams(dimension_semantics=("arbitrary", "arbitrary"),
                                             vmem_limit_bytes=VMEM_LIMIT),
        name="outproj",
    )(x, oa, ga, ob, gb, w_out.astype(BF16), g_post.reshape(1, D_MODEL), gate)


def _layer(x, mod, pos_off, tabs, wp, g_pre, w_out, g_post, past):
    b, t, _ = x.shape
    shift, scale, gate = jnp.split(mod, 3, axis=-1)
    p = _inproj(x, (1.0 + scale)[:, None, :], shift[:, None, :], g_pre, wp, tabs)
    ka, va, kiw = p["ka"], p["va"], p["kiw"]
    if past is None:
        k_all, v_all, kiw_all = ka, va, kiw
        s0 = jnp.zeros((b, N_HEADS_B, HD_B, HD_B), F32)
    else:
        pk, pv, pki, s0 = past
        plen = pk.shape[1]
        k_all = jnp.concatenate([pk.reshape(b, plen, N_KV_A * HD_A), ka], axis=1)
        v_all = jnp.concatenate([pv.reshape(b, plen, N_KV_A * HD_A), va], axis=1)
        pki = jnp.pad(pki, ((0, 0), (0, 0), (0, LANES - IDX_DIM)))
        kiw_all = jnp.concatenate([pki, kiw], axis=1)
    s_real = k_all.shape[1]
    s_pad = -(-s_real // SEL_BLK) * SEL_BLK
    t_pad = -(-t // QB) * QB
    kk, kiw_k, vt = _attn_operands(k_all, v_all, kiw_all, s_pad)
    oa = _attention(_pad_rows(p["qa"], t_pad), _pad_rows(p["qi"], t_pad), _pad_rows(kiw, t_pad),
                    kk, kiw_k, vt, pos_off, s_real)[:, :t]
    ob, s_new = _retention(p["qb"], p["kb"], p["vb"], s0)
    x_new = _outproj(x, oa, p["ga"], ob, p["gb"], w_out, g_post, gate[:, None, :])
    k_n = ka.reshape(b, t, N_KV_A, HD_A)
    v_n = va.reshape(b, t, N_KV_A, HD_A)
    return x_new, (k_n, v_n, kiw[..., :IDX_DIM], s_new)


def kernel(x_prompt, x_sample, cache_k_a, cache_v_a, cache_kidx_a, state_s_b, c_prompt, c_sample,
           w_ada, b_ada, g_pre, w_in, w_out, g_post):
    depth = w_in.shape[0]
    nb = x_prompt.shape[0]
    past_len = cache_k_a.shape[2]
    tabs_p = _rope_tables(jnp.arange(x_prompt.shape[1], dtype=jnp.int32))
    tabs_s = _rope_tables(past_len + jnp.arange(x_sample.shape[1], dtype=jnp.int32))
    mod = _modulation(jnp.concatenate([c_prompt, c_sample], axis=0), w_ada, b_ada)
    hp, hs = x_prompt, x_sample
    outs_p, outs_s = [], []
    for l in range(depth):
        wp = _prep_w_in(w_in[l])
        hp, new_p = _layer(hp, mod[l, :nb], 0, tabs_p, wp, g_pre[l], w_out[l], g_post[l], None)
        outs_p.append(new_p)
        hs, new_s = _layer(hs, mod[l, nb:], past_len, tabs_s, wp, g_pre[l], w_out[l], g_post[l],
                           (cache_k_a[l], cache_v_a[l], cache_kidx_a[l], state_s_b[l]))
        outs_s.append(new_s)
    stack = lambda outs, i: jnp.stack([o[i] for o in outs])
    return (hp, hs,
            stack(outs_p, 0), stack(outs_p, 1), stack(outs_p, 2), stack(outs_p, 3),
            stack(outs_s, 0), stack(outs_s, 1), stack(outs_s, 2), stack(outs_s, 3))
```

```python
import functools
import math

import jax
import jax.numpy as jnp
from jax import lax
from jax.experimental import pallas as pl
from jax.experimental.pallas import tpu as pltpu

D_MODEL = 1024
CHUNK = 64
W_A = 512
HD_A = 64
N_HEADS_A = 8
N_KV_A = 2
N_GROUP_A = N_HEADS_A // N_KV_A
N_IDX = 4
IDX_DIM = 64
MAX_TOPK = 256
HD_B = 128
N_HEADS_B = 4
W_B = 512
ROPE_THETA = 10000.0
EPS = 1e-6
IDX_SCALE = (IDX_DIM ** -0.5) * (N_IDX ** -0.5)
LOG2E = math.log2(math.e)

LANES = 128
PACK16 = 16
VMEM_LIMIT = 56 * 1024 * 1024
INT_MIN = -(2 ** 31)
NEG = -1e30

QB = LANES
SEL_BLK = 512
ATT_BLK = 256
SEL_GRP = 8
VROWS = HD_A + PACK16

F32 = jnp.float32
BF16 = jnp.bfloat16
I32 = jnp.int32

_SEGS = (
    ("qa", W_A, "a"),
    ("ka", N_KV_A * HD_A, "a"),
    ("va", N_KV_A * HD_A, None),
    ("ga", W_A, None),
    ("qi", N_IDX * IDX_DIM, "a"),
    ("kiw", LANES, "k"),
    ("qb", W_B, "b"),
    ("kb", W_B, "b"),
    ("vb", W_B, None),
    ("gb", W_B, None),
)
_W_COLS = sum(s[1] for s in _SEGS)
_OUT_DTYPES = {"qa": BF16, "ka": F32, "va": F32, "ga": BF16, "qi": BF16, "kiw": F32,
               "qb": BF16, "kb": BF16, "vb": BF16, "gb": BF16}


def _silu(x):
    return x / (1.0 + jnp.exp(-x))


def _fold(op, x):
    parts = [x[i] for i in range(x.shape[0])]
    while len(parts) > 1:
        parts = [op(parts[i], parts[i + 1]) for i in range(0, len(parts) - 1, 2)] + parts[len(parts) & ~1:]
    return parts[0]


def _mod_kernel(c_ref, w_ref, b_ref, o_ref):
    c = c_ref[...]
    s = _silu(c)
    o_ref[0] = jnp.dot(s.astype(BF16), w_ref[0].astype(BF16), preferred_element_type=F32) + b_ref[0]


def _modulation(c_all, w_ada, b_ada):
    depth = w_ada.shape[0]
    n = c_all.shape[0]
    nblk = 3
    return pl.pallas_call(
        _mod_kernel,
        grid=(depth, nblk),
        in_specs=[
            pl.BlockSpec((n, D_MODEL), lambda l, j: (0, 0)),
            pl.BlockSpec((1, D_MODEL, D_MODEL), lambda l, j: (l, 0, j)),
            pl.BlockSpec((1, 1, D_MODEL), lambda l, j: (l, 0, j)),
        ],
        out_specs=pl.BlockSpec((1, n, D_MODEL), lambda l, j: (l, 0, j)),
        out_shape=jax.ShapeDtypeStruct((depth, n, 3 * D_MODEL), F32),
        compiler_params=pltpu.CompilerParams(dimension_semantics=("arbitrary", "arbitrary"),
                                             vmem_limit_bytes=VMEM_LIMIT),
        name="modulation",
    )(c_all, w_ada, b_ada.reshape(depth, 1, 3 * D_MODEL))


def _rope_slab(p, kind, tabs):
    if kind == "b":
        cos, sin_s = tabs["b"]
        return p * cos[...] + pltpu.roll(p, HD_B // 2, axis=1) * sin_s[...]
    cos, sin_lo, sin_hi = tabs[kind]
    return (p * cos[...] + pltpu.roll(p, LANES - HD_A // 2, axis=1) * sin_lo[...]
            + pltpu.roll(p, HD_A // 2, axis=1) * sin_hi[...])


def _inproj_kernel(x_ref, sc_ref, sh_ref, g_ref, w_ref,
                   ca_ref, sal_ref, sah_ref, ck_ref, skl_ref, skh_ref, cb_ref, sbs_ref, *out_refs):
    x = x_ref[0]
    ms = jnp.mean(x * x, axis=-1, keepdims=True)
    y = x * lax.rsqrt(ms + EPS) * g_ref[...]
    h = y * sc_ref[0] + sh_ref[0]
    hb = h.astype(BF16)
    tabs = {"a": (ca_ref, sal_ref, sah_ref), "k": (ck_ref, skl_ref, skh_ref), "b": (cb_ref, sbs_ref)}
    c0 = 0
    for (name, width, kind), o_ref in zip(_SEGS, out_refs):
        step = 2 * LANES if width % (2 * LANES) == 0 else LANES
        for off in range(0, width, step):
            p = jnp.dot(hb, w_ref[:, c0 + off:c0 + off + step], preferred_element_type=F32)
            for s in range(step // LANES):
                ps = p[:, s * LANES:(s + 1) * LANES]
                if kind is not None:
                    ps = _rope_slab(ps, kind, tabs)
                if name == "kb":
                    ps = ps * (HD_B ** -0.5)
                if name in ("ga", "gb"):
                    ps = _silu(ps)
                o_ref[0, :, off + s * LANES:off + (s + 1) * LANES] = ps.astype(o_ref.dtype)
        c0 += width


def _prep_w_in(w):
    sizes = (W_A, N_KV_A * HD_A, N_KV_A * HD_A, W_A, N_IDX * IDX_DIM, IDX_DIM, N_IDX, W_B, W_B, W_B, W_B)
    parts = []
    o = 0
    for n in sizes:
        parts.append(w[:, o:o + n])
        o += n
    q_a, k_a, v_a, g_a, q_i, k_i, w_i, q_b, k_b, v_b, g_b = parts
    kk = w.shape[0]
    q_a = q_a * ((HD_A ** -0.5) * LOG2E)
    kiw = jnp.concatenate([k_i, w_i, jnp.zeros((kk, LANES - IDX_DIM - N_IDX), w.dtype)], axis=1)
    segs = {"qa": q_a, "ka": k_a, "va": v_a, "ga": g_a, "qi": q_i, "kiw": kiw,
            "qb": q_b, "kb": k_b, "vb": v_b, "gb": g_b}
    return jnp.concatenate([segs[s[0]] for s in _SEGS], axis=1).astype(BF16)


def _rope_tables(pos):
    posf = pos.astype(F32)[:, None]
    lane = jnp.arange(LANES)[None, :]

    def tab(d):
        half = d // 2
        inv = ROPE_THETA ** (-jnp.arange(half, dtype=F32) * 2.0 / d)
        ang = posf * inv[None, :]
        reps = LANES // half
        return jnp.tile(jnp.cos(ang), (1, reps)), jnp.tile(jnp.sin(ang), (1, reps)), (lane % d) < half

    ca, sa, lo_a = tab(HD_A)
    sal, sah = jnp.where(lo_a, -sa, 0.0), jnp.where(lo_a, 0.0, sa)
    in_k = lane < IDX_DIM
    ck, skl, skh = jnp.where(in_k, ca, 1.0), jnp.where(in_k, sal, 0.0), jnp.where(in_k, sah, 0.0)
    cb, sb, lo_b = tab(HD_B)
    return ca, sal, sah, ck, skl, skh, cb, jnp.where(lo_b, -sb, sb)


def _row_tile(t):
    return min(512, t)


def _inproj(x, scale1p, shift, g_pre, wp, tabs):
    b, t, _ = x.shape
    tm = _row_tile(t)
    nt = t // tm
    const = lambda i, j: (0, 0)
    tab_spec = pl.BlockSpec((tm, LANES), lambda i, j: (j, 0))
    in_specs = [
        pl.BlockSpec((1, tm, D_MODEL), lambda i, j: (i, j, 0)),
        pl.BlockSpec((1, 1, D_MODEL), lambda i, j: (i, 0, 0)),
        pl.BlockSpec((1, 1, D_MODEL), lambda i, j: (i, 0, 0)),
        pl.BlockSpec((1, D_MODEL), const),
        pl.BlockSpec((D_MODEL, _W_COLS), const, pipeline_mode=pl.Buffered(1)),
    ] + [tab_spec] * len(tabs)
    out_specs = [pl.BlockSpec((1, tm, s[1]), lambda i, j: (i, j, 0)) for s in _SEGS]
    out_shape = [jax.ShapeDtypeStruct((b, t, s[1]), _OUT_DTYPES[s[0]]) for s in _SEGS]
    outs = pl.pallas_call(
        _inproj_kernel,
        grid=(b, nt),
        in_specs=in_specs,
        out_specs=out_specs,
        out_shape=out_shape,
        compiler_params=pltpu.CompilerParams(dimension_semantics=("arbitrary", "arbitrary"),
                                             vmem_limit_bytes=VMEM_LIMIT),
        name="inproj",
    )(x, scale1p, shift, g_pre.reshape(1, D_MODEL), wp, *tabs)
    return dict(zip([s[0] for s in _SEGS], outs))


def _attn_kernel(qa_ref, qi_ref, wq_ref, kk_ref, kiw_ref, vt_ref, tri_ref, o_ref,
                 keys_ref, planes_ref, alive_ref, qt_ref, qit_ref, m_ref, acc_ref, sb_ref, x_ref,
                 *, pos_off, topk, s_real):
    j = pl.program_id(1)
    npair = N_HEADS_A // 2

    qlane = lax.broadcasted_iota(I32, (1, QB), 1)
    pos = pos_off + j * QB + qlane
    lim = jnp.minimum((lax.shift_right_logical(pos, 6) + 1) * CHUNK, s_real)
    last_pos = pos_off + (j + 1) * QB - 1
    lim_max = jnp.minimum((lax.shift_right_logical(last_pos, 6) + 1) * CHUNK, s_real)
    n_sel = lax.div(lim_max + (SEL_BLK - 1), SEL_BLK)
    n_att = n_sel * (SEL_BLK // ATT_BLK)

    zeros_half = jnp.zeros((HD_A, QB), BF16)
    for s in range(N_HEADS_A // 2):
        t = qa_ref[0, :, s * LANES:(s + 1) * LANES].astype(F32).T.astype(BF16)
        n = (2 * s) // N_GROUP_A
        for c in range(2):
            cols = slice((2 * s + c) * LANES, (2 * s + c + 1) * LANES)
            qt_ref[n * HD_A:(n + 1) * HD_A, cols] = t[c * HD_A:(c + 1) * HD_A, :]
            qt_ref[(1 - n) * HD_A:(2 - n) * HD_A, cols] = zeros_half
    for s in range(N_IDX // 2):
        t = qi_ref[0, :, s * LANES:(s + 1) * LANES].astype(F32).T.astype(BF16)
        for c in range(2):
            cols = slice((2 * s + c) * LANES, (2 * s + c + 1) * LANES)
            qit_ref[:IDX_DIM, cols] = t[c * IDX_DIM:(c + 1) * IDX_DIM, :]
            qit_ref[IDX_DIM:, cols] = zeros_half
    wt = wq_ref[0].T
    w_idx = [wt[IDX_DIM + h:IDX_DIM + h + 1, :] * IDX_SCALE for h in range(N_IDX)]

    n_blk = n_sel * (SEL_BLK // ATT_BLK)
    vregs = ATT_BLK // 8

    def score_dots(blk, slot):
        kblk = kiw_ref[0, pl.ds(pl.multiple_of(blk * ATT_BLK, ATT_BLK), ATT_BLK), :]
        for m in range(N_IDX // 2):
            x = jnp.dot(kblk, qit_ref[:, 2 * m * LANES:(2 * m + 2) * LANES], preferred_element_type=F32)
            x_ref[slot, 2 * m] = x[:, :LANES]
            x_ref[slot, 2 * m + 1] = x[:, LANES:]

    def score_keys(blk, slot):
        off = pl.multiple_of(blk * ATT_BLK, ATT_BLK)
        sc = None
        for h in range(N_IDX):
            term = jnp.maximum(x_ref[slot, h], 0.0) * w_idx[h]
            sc = term if sc is None else sc + term
        bits = pltpu.bitcast(sc, I32)
        key = bits ^ (lax.shift_right_arithmetic(bits, 31) & 0x7FFFFFFF)
        kpos = off + lax.broadcasted_iota(I32, (ATT_BLK, QB), 0)
        key = jnp.where(kpos < lim, key, INT_MIN)
        keys_ref[pl.ds(off, ATT_BLK), :] = key
        a = [key[8 * r:8 * r + 8, :] ^ INT_MIN for r in range(vregs)]
        for d, msk in ((16, 0x0000FFFF), (8, 0x00FF00FF), (4, 0x0F0F0F0F), (2, 0x33333333), (1, 0x55555555)):
            for k in range(vregs):
                if k & d == 0:
                    t = (a[k] ^ lax.shift_right_logical(a[k + d], d)) & msk
                    a[k] = a[k] ^ t
                    a[k + d] = a[k + d] ^ lax.shift_left(t, d)
        for r in range(vregs):
            planes_ref[r, blk] = a[r]

    score_dots(0, 0)

    def score_two_blocks(i, carry):
        score_dots(2 * i + 1, 1)
        score_keys(2 * i, 0)
        score_dots(jnp.minimum(2 * i + 2, n_blk - 1), 0)
        score_keys(2 * i + 1, 1)
        return carry

    lax.fori_loop(0, n_sel, score_two_blocks, 0)

    alive_ref[...] = jnp.full(alive_ref.shape, -1, I32)

    n_grp = lax.div(n_blk + (SEL_GRP - 1), SEL_GRP)

    def zero_blk(i, carry):
        for r in range(vregs):
            planes_ref[r, i] = jnp.zeros((8, QB), I32)
        return carry

    lax.fori_loop(n_blk, n_grp * SEL_GRP, zero_blk, 0)

    def count_ones(p_cur, p_prev, flip):
        def grp(i, acc):
            blks = pl.ds(pl.multiple_of(i * SEL_GRP, SEL_GRP), SEL_GRP)
            alive = alive_ref[blks]
            if p_prev is not None:
                alive = alive & (planes_ref[p_prev, blks] ^ flip[None])
                alive_ref[blks] = alive
            return acc + _fold(jnp.add, lax.population_count(alive & planes_ref[p_cur, blks]))

        acc = lax.fori_loop(0, n_grp, grp, jnp.zeros((8, QB), I32))
        return jnp.sum(acc, axis=0, keepdims=True)

    def decide(p, ones, st):
        thr_u, rem, _ = st
        take = ones >= rem
        bit = lax.shift_left(jnp.int32(1), 31 - p)
        return (jnp.where(take, thr_u | bit, thr_u), jnp.where(take, rem, rem - ones),
                jnp.where(take, 0, -1))

    zero = jnp.zeros((1, QB), I32)
    st = decide(0, count_ones(0, None, None), (zero, zero + topk, zero))
    st = lax.fori_loop(1, 32, lambda p, s: decide(p, count_ones(p, p - 1, s[2]), s), st)
    thr_u, need, flip = st

    def eq_grp(i, acc):
        blks = pl.ds(pl.multiple_of(i * SEL_GRP, SEL_GRP), SEL_GRP)
        return acc + _fold(jnp.add, lax.population_count(alive_ref[blks] & (planes_ref[31, blks] ^ flip[None])))

    n_eq = jnp.sum(lax.fori_loop(0, n_grp, eq_grp, jnp.zeros((8, QB), I32)), axis=0, keepdims=True)
    thr = jnp.maximum(thr_u ^ INT_MIN, INT_MIN + 1)
    tie_cut = jnp.max(jnp.where((n_eq > need) & (thr_u != 0), 1, 0)) > 0

    @pl.when(tie_cut)
    def _():
        need_f = need.astype(F32)

        def blk(i, run):
            off = pl.multiple_of(i * ATT_BLK, ATT_BLK)
            key = keys_ref[pl.ds(off, ATT_BLK), :]
            eq = key == thr
            pre = jnp.dot(tri_ref[...], jnp.where(eq, 1.0, 0.0).astype(BF16), preferred_element_type=F32)
            keep_eq = (pre + run) <= need_f
            keys_ref[pl.ds(off, ATT_BLK), :] = jnp.where(eq, jnp.where(keep_eq, key, INT_MIN), key)
            return run + pre[ATT_BLK - 1:ATT_BLK, :]

        lax.fori_loop(0, n_att, blk, jnp.zeros((1, QB), F32))

    m_ref[...] = jnp.full(m_ref.shape, NEG, F32)
    acc_ref[...] = jnp.zeros(acc_ref.shape, F32)

    def stage_logits(off, slot):
        kblk = kk_ref[0, pl.ds(off, ATT_BLK), :]
        bias = jnp.where(keys_ref[pl.ds(off, ATT_BLK), :] >= thr, 0.0, NEG).astype(BF16)
        bias2 = jnp.concatenate([bias, bias], axis=1)
        parts = []
        for m in range(npair):
            s = jnp.dot(kblk, qt_ref[:, 2 * m * LANES:(2 * m + 2) * LANES], preferred_element_type=F32)
            sb = s.astype(BF16) + bias2
            sb_ref[slot, m] = sb
            parts.append(_fold(jnp.maximum, sb.reshape(ATT_BLK // PACK16, PACK16, 2 * QB)))
        return parts

    def stage_probs(parts, slot):
        alphas = []
        for m in range(npair):
            m_prev = m_ref[m]
            m_new = jnp.maximum(m_prev, jnp.max(parts[m].astype(F32), axis=0, keepdims=True))
            alphas.append(jnp.exp2(m_prev - m_new))
            sb_ref[slot, m] = jnp.exp2(sb_ref[slot, m] - m_new.astype(BF16))
            m_ref[m] = m_new
        return alphas

    def stage_values(off, slot, alphas):
        for m in range(npair):
            n = (2 * m) // N_GROUP_A
            pv = jnp.dot(vt_ref[0, n * VROWS:(n + 1) * VROWS, pl.ds(off, ATT_BLK)], sb_ref[slot, m],
                         preferred_element_type=F32)
            acc_ref[m] = alphas[m] * acc_ref[m] + pv

    sb_ref[1] = jnp.full(sb_ref.shape[1:], -jnp.inf, BF16)

    def attn_two_blocks(i, parts_prev):
        off0 = pl.multiple_of(i * SEL_BLK, SEL_BLK)
        off1 = pl.multiple_of(off0 + ATT_BLK, ATT_BLK)
        off_prev = pl.multiple_of(jnp.maximum(off0 - ATT_BLK, 0), ATT_BLK)
        parts0 = stage_logits(off0, 0)
        alphas_prev = stage_probs(parts_prev, 1)
        stage_values(off_prev, 1, alphas_prev)
        alphas0 = stage_probs(parts0, 0)
        parts1 = stage_logits(off1, 1)
        stage_values(off0, 0, alphas0)
        return tuple(parts1)

    neg_parts = tuple(jnp.full((PACK16, 2 * QB), NEG, BF16) for _ in range(npair))
    parts_last = lax.fori_loop(0, n_sel, attn_two_blocks, neg_parts)
    alphas_last = stage_probs(parts_last, 1)
    stage_values(pl.multiple_of(n_sel * SEL_BLK - ATT_BLK, ATT_BLK), 1, alphas_last)

    heads = []
    for m in range(npair):
        a = acc_ref[m]
        o = a[:HD_A, :] / a[HD_A:HD_A + 1, :]
        heads += [o[:, :QB], o[:, QB:]]
    o_ref[0] = jnp.concatenate(heads, axis=0).T.astype(o_ref.dtype)


def _attention(qa, qi, kiw_q, kk, kiw_k, vt, pos_off, s_real):
    b, t, _ = qa.shape
    s_pad = kk.shape[1]
    topk = min(MAX_TOPK, s_real // 4)
    idx = jnp.arange(ATT_BLK)
    tri = (idx[None, :] <= idx[:, None]).astype(BF16)
    kern = functools.partial(_attn_kernel, pos_off=pos_off, topk=topk, s_real=s_real)
    npair = N_HEADS_A // 2
    nblk_pad = -(-(s_pad // ATT_BLK) // SEL_GRP) * SEL_GRP
    return pl.pallas_call(
        kern,
        grid=(b, t // QB),
        in_specs=[
            pl.BlockSpec((1, QB, W_A), lambda i, j: (i, j, 0)),
            pl.BlockSpec((1, QB, N_IDX * IDX_DIM), lambda i, j: (i, j, 0)),
            pl.BlockSpec((1, QB, LANES), lambda i, j: (i, j, 0)),
            pl.BlockSpec((1, s_pad, LANES), lambda i, j: (i, 0, 0)),
            pl.BlockSpec((1, s_pad, LANES), lambda i, j: (i, 0, 0)),
            pl.BlockSpec((1, N_KV_A * VROWS, s_pad), lambda i, j: (i, 0, 0)),
            pl.BlockSpec((ATT_BLK, ATT_BLK), lambda i, j: (0, 0)),
        ],
        out_specs=pl.BlockSpec((1, QB, W_A), lambda i, j: (i, j, 0)),
        out_shape=jax.ShapeDtypeStruct((b, t, W_A), BF16),
        scratch_shapes=[
            pltpu.VMEM((s_pad, QB), I32),
            pltpu.VMEM((ATT_BLK // 8, nblk_pad, 8, QB), I32),
            pltpu.VMEM((nblk_pad, 8, QB), I32),
            pltpu.VMEM((LANES, N_HEADS_A * LANES), BF16),
            pltpu.VMEM((LANES, N_IDX * LANES), BF16),
            pltpu.VMEM((npair, 1, 2 * QB), F32),
            pltpu.VMEM((npair, VROWS, 2 * QB), F32),
            pltpu.VMEM((2, npair, ATT_BLK, 2 * QB), BF16),
            pltpu.VMEM((2, N_IDX, ATT_BLK, QB), F32),
        ],
        compiler_params=pltpu.CompilerParams(dimension_semantics=("arbitrary", "arbitrary"),
                                             vmem_limit_bytes=VMEM_LIMIT),
        name="dsa_attention",
    )(qa, qi, kiw_q, kk, kiw_k, vt, tri)


def _attn_operands(ka, va, kiw, s_pad):
    b, s, _ = ka.shape
    pad = ((0, 0), (0, s_pad - s), (0, 0))
    kk = jnp.pad(ka.astype(BF16), pad)
    kiw_k = jnp.pad(kiw.astype(BF16), pad)
    vb = jnp.pad(va.astype(BF16), pad).reshape(b, s_pad, N_KV_A, HD_A)
    ones = jnp.ones((b, s_pad, N_KV_A, VROWS - HD_A), BF16)
    vaug = jnp.concatenate([vb, ones], axis=-1).reshape(b, s_pad, N_KV_A * VROWS)
    return kk, kiw_k, jnp.swapaxes(vaug, 1, 2)


def _pad_rows(a, t_pad):
    return jnp.pad(a, ((0, 0), (0, t_pad - a.shape[1]), (0, 0)))


def _ret_kernel(q_ref, k_ref, v_ref, s0_ref, dm_ref, qd_ref, kd_ref, sd_ref, o_ref, sn_ref, s_scr,
                *, rt, rc):
    t = pl.program_id(1)

    @pl.when(t == 0)
    def _():
        s_scr[...] = s0_ref[0]

    heads = range(N_HEADS_B)
    nt_dims = (((1,), (1,)), ((), ()))
    tn_dims = (((0,), (0,)), ((), ()))
    for c in range(rt // rc):
        rows = slice(c * rc, (c + 1) * rc)
        q = [q_ref[0, rows, h * HD_B:(h + 1) * HD_B] for h in heads]
        k = [k_ref[0, rows, h * HD_B:(h + 1) * HD_B] for h in heads]
        v = [v_ref[0, rows, h * HD_B:(h + 1) * HD_B] for h in heads]
        s = [s_scr[h] for h in heads]
        att = [lax.dot_general(q[h], k[h], nt_dims, preferred_element_type=F32) * dm_ref[h] for h in heads]
        qs = [jnp.dot(q[h], s[h].astype(BF16), preferred_element_type=F32) * qd_ref[h] for h in heads]
        kd = [(k[h].astype(F32) * kd_ref[h]).astype(BF16) for h in heads]
        kv = [lax.dot_general(kd[h], v[h], tn_dims, preferred_element_type=F32) for h in heads]
        o = [jnp.dot(att[h].astype(BF16), v[h], preferred_element_type=F32) + qs[h] for h in heads]
        for h in heads:
            s_scr[h] = s[h] * sd_ref[h] + kv[h]
            mu = jnp.mean(o[h], axis=-1, keepdims=True)
            d = o[h] - mu
            var = jnp.mean(d * d, axis=-1, keepdims=True)
            o_ref[0, rows, h * HD_B:(h + 1) * HD_B] = (d * lax.rsqrt(var + EPS)).astype(o_ref.dtype)

    @pl.when(t == pl.num_programs(1) - 1)
    def _():
        sn_ref[0] = s_scr[...]


def _retention(qb_, kb_, vb_, s0):
    b, t, _ = qb_.shape
    rt = min(512, t)
    rc = min(128, t)
    lg = jnp.log1p(-jnp.exp2(-5.0 - jnp.arange(N_HEADS_B, dtype=F32)))
    i = jnp.arange(rc, dtype=F32)
    diff = i[:, None] - i[None, :]
    dmask = jnp.exp(jnp.where(diff >= 0, diff[None] * lg[:, None, None], -jnp.inf))
    ones = jnp.ones((1, 1, HD_B), F32)
    q_dec = jnp.exp((i + 1.0)[None, :] * lg[:, None])[:, :, None] * ones
    k_dec = jnp.exp((rc - 1.0 - i)[None, :] * lg[:, None])[:, :, None] * ones
    s_dec = jnp.exp(rc * lg)[:, None, None] * jnp.ones((1, HD_B, HD_B), F32)
    blk = pl.BlockSpec((1, rt, W_B), lambda i_, j: (i_, j, 0))
    st = pl.BlockSpec((1, N_HEADS_B, HD_B, HD_B), lambda i_, j: (i_, 0, 0, 0))
    c3 = lambda i_, j: (0, 0, 0)
    kern = functools.partial(_ret_kernel, rt=rt, rc=rc)
    return pl.pallas_call(
        kern,
        grid=(b, t // rt),
        in_specs=[blk, blk, blk, st,
                  pl.BlockSpec((N_HEADS_B, rc, rc), c3),
                  pl.BlockSpec((N_HEADS_B, rc, HD_B), c3),
                  pl.BlockSpec((N_HEADS_B, rc, HD_B), c3),
                  pl.BlockSpec((N_HEADS_B, HD_B, HD_B), c3)],
        out_specs=[blk, st],
        out_shape=[jax.ShapeDtypeStruct((b, t, W_B), BF16),
                   jax.ShapeDtypeStruct((b, N_HEADS_B, HD_B, HD_B), F32)],
        scratch_shapes=[pltpu.VMEM((N_HEADS_B, HD_B, HD_B), F32)],
        compiler_params=pltpu.CompilerParams(dimension_semantics=("arbitrary", "arbitrary"),
                                             vmem_limit_bytes=VMEM_LIMIT),
        name="retention",
    )(qb_, kb_, vb_, s0, dmask, q_dec, k_dec, s_dec)


def _outproj_kernel(x_ref, oa_ref, ga_ref, ob_ref, gb_ref, w_ref, g_ref, gate_ref, o_ref):
    ma = (oa_ref[0].astype(F32) * ga_ref[0].astype(F32)).astype(BF16)
    mb = (ob_ref[0].astype(F32) * gb_ref[0].astype(F32)).astype(BF16)
    y = (jnp.dot(ma, w_ref[:W_A, :], preferred_element_type=F32)
         + jnp.dot(mb, w_ref[W_A:, :], preferred_element_type=F32))
    ms = jnp.mean(y * y, axis=-1, keepdims=True)
    yn = y * lax.rsqrt(ms + EPS) * g_ref[...]
    o_ref[0] = x_ref[0] + gate_ref[0] * yn


def _outproj(x, oa, ga, ob, gb, w_out, g_post, gate):
    b, t, _ = x.shape
    tm = _row_tile(t)
    row = lambda w: pl.BlockSpec((1, tm, w), lambda i, j: (i, j, 0))
    return pl.pallas_call(
        _outproj_kernel,
        grid=(b, t // tm),
        in_specs=[row(D_MODEL), row(W_A), row(W_A), row(W_B), row(W_B),
                  pl.BlockSpec((D_MODEL, D_MODEL), lambda i, j: (0, 0)),
                  pl.BlockSpec((1, D_MODEL), lambda i, j: (0, 0)),
                  pl.BlockSpec((1, 1, D_MODEL), lambda i, j: (i, 0, 0))],
        out_specs=row(D_MODEL),
        out_shape=jax.ShapeDtypeStruct((b, t, D_MODEL), F32),
        compiler_params=pltpu.CompilerParams(dimension_semantics=("arbitrary", "arbitrary"),
                                             vmem_limit_bytes=VMEM_LIMIT),
        name="outproj",
    )(x, oa, ga, ob, gb, w_out.astype(BF16), g_post.reshape(1, D_MODEL), gate)


def _layer(x, mod, pos_off, tabs, wp, g_pre, w_out, g_post, past):
    b, t, _ = x.shape
    shift, scale, gate = jnp.split(mod, 3, axis=-1)
    p = _inproj(x, (1.0 + scale)[:, None, :], shift[:, None, :], g_pre, wp, tabs)
    ka, va, kiw = p["ka"], p["va"], p["kiw"]
    if past is None:
        k_all, v_all, kiw_all = ka, va, kiw
        s0 = jnp.zeros((b, N_HEADS_B, HD_B, HD_B), F32)
    else:
        pk, pv, pki, s0 = past
        plen = pk.shape[1]
        k_all = jnp.concatenate([pk.reshape(b, plen, N_KV_A * HD_A), ka], axis=1)
        v_all = jnp.concatenate([pv.reshape(b, plen, N_KV_A * HD_A), va], axis=1)
        pki = jnp.pad(pki, ((0, 0), (0, 0), (0, LANES - IDX_DIM)))
        kiw_all = jnp.concatenate([pki, kiw], axis=1)
    s_real = k_all.shape[1]
    s_pad = -(-s_real // SEL_BLK) * SEL_BLK
    t_pad = -(-t // QB) * QB
    kk, kiw_k, vt = _attn_operands(k_all, v_all, kiw_all, s_pad)
    oa = _attention(_pad_rows(p["qa"], t_pad), _pad_rows(p["qi"], t_pad), _pad_rows(kiw, t_pad),
                    kk, kiw_k, vt, pos_off, s_real)[:, :t]
    ob, s_new = _retention(p["qb"], p["kb"], p["vb"], s0)
    x_new = _outproj(x, oa, p["ga"], ob, p["gb"], w_out, g_post, gate[:, None, :])
    k_n = ka.reshape(b, t, N_KV_A, HD_A)
    v_n = va.reshape(b, t, N_KV_A, HD_A)
    return x_new, (k_n, v_n, kiw[..., :IDX_DIM], s_new)


def kernel(x_prompt, x_sample, cache_k_a, cache_v_a, cache_kidx_a, state_s_b, c_prompt, c_sample,
           w_ada, b_ada, g_pre, w_in, w_out, g_post):
    depth = w_in.shape[0]
    nb = x_prompt.shape[0]
    past_len = cache_k_a.shape[2]
    tabs_p = _rope_tables(jnp.arange(x_prompt.shape[1], dtype=jnp.int32))
    tabs_s = _rope_tables(past_len + jnp.arange(x_sample.shape[1], dtype=jnp.int32))
    mod = _modulation(jnp.concatenate([c_prompt, c_sample], axis=0), w_ada, b_ada)
    hp, hs = x_prompt, x_sample
    outs_p, outs_s = [], []
    for l in range(depth):
        wp = _prep_w_in(w_in[l])
        hp, new_p = _layer(hp, mod[l, :nb], 0, tabs_p, wp, g_pre[l], w_out[l], g_post[l], None)
        outs_p.append(new_p)
        hs, new_s = _layer(hs, mod[l, nb:], past_len, tabs_s, wp, g_pre[l], w_out[l], g_post[l],
                           (cache_k_a[l], cache_v_a[l], cache_kidx_a[l], state_s_b[l]))
        outs_s.append(new_s)
    stack = lambda outs, i: jnp.stack([o[i] for o in outs])
    return (hp, hs,
            stack(outs_p, 0), stack(outs_p, 1), stack(outs_p, 2), stack(outs_p, 3),
            stack(outs_s, 0), stack(outs_s, 1), stack(outs_s, 2), stack(outs_s, 3))
```

```python
import functools
import math

import jax
import jax.numpy as jnp
from jax import lax
from jax.experimental import pallas as pl
from jax.experimental.pallas import tpu as pltpu

D_MODEL = 1024
CHUNK = 64
W_A = 512
HD_A = 64
N_HEADS_A = 8
N_KV_A = 2
N_GROUP_A = N_HEADS_A // N_KV_A
N_IDX = 4
IDX_DIM = 64
MAX_TOPK = 256
HD_B = 128
N_HEADS_B = 4
W_B = 512
ROPE_THETA = 10000.0
EPS = 1e-6
IDX_SCALE = (IDX_DIM ** -0.5) * (N_IDX ** -0.5)
LOG2E = math.log2(math.e)

LANES = 128
PACK16 = 16
VMEM_LIMIT = 56 * 1024 * 1024
INT_MIN = -(2 ** 31)
NEG = -1e30

QB = LANES
SEL_BLK = 512
ATT_BLK = 256
SEL_GRP = 16
VROWS = HD_A + PACK16

F32 = jnp.float32
BF16 = jnp.bfloat16
I32 = jnp.int32

_SEGS = (
    ("qa", W_A, "a"),
    ("ka", N_KV_A * HD_A, "a"),
    ("va", N_KV_A * HD_A, None),
    ("ga", W_A, None),
    ("qi", N_IDX * IDX_DIM, "a"),
    ("kiw", LANES, "k"),
    ("qb", W_B, "b"),
    ("kb", W_B, "b"),
    ("vb", W_B, None),
    ("gb", W_B, None),
)
_W_COLS = sum(s[1] for s in _SEGS)
_OUT_DTYPES = {"qa": BF16, "ka": F32, "va": F32, "ga": BF16, "qi": BF16, "kiw": F32,
               "qb": BF16, "kb": BF16, "vb": BF16, "gb": BF16}


def _silu(x):
    return x / (1.0 + jnp.exp(-x))


def _fold(op, x):
    parts = [x[i] for i in range(x.shape[0])]
    while len(parts) > 1:
        parts = [op(parts[i], parts[i + 1]) for i in range(0, len(parts) - 1, 2)] + parts[len(parts) & ~1:]
    return parts[0]


def _mod_kernel(c_ref, w_ref, b_ref, o_ref):
    c = c_ref[...]
    s = _silu(c)
    o_ref[0] = jnp.dot(s.astype(BF16), w_ref[0].astype(BF16), preferred_element_type=F32) + b_ref[0]


def _modulation(c_all, w_ada, b_ada):
    depth = w_ada.shape[0]
    n = c_all.shape[0]
    nblk = 3
    return pl.pallas_call(
        _mod_kernel,
        grid=(depth, nblk),
        in_specs=[
            pl.BlockSpec((n, D_MODEL), lambda l, j: (0, 0)),
            pl.BlockSpec((1, D_MODEL, D_MODEL), lambda l, j: (l, 0, j)),
            pl.BlockSpec((1, 1, D_MODEL), lambda l, j: (l, 0, j)),
        ],
        out_specs=pl.BlockSpec((1, n, D_MODEL), lambda l, j: (l, 0, j)),
        out_shape=jax.ShapeDtypeStruct((depth, n, 3 * D_MODEL), F32),
        compiler_params=pltpu.CompilerParams(dimension_semantics=("arbitrary", "arbitrary"),
                                             vmem_limit_bytes=VMEM_LIMIT),
        name="modulation",
    )(c_all, w_ada, b_ada.reshape(depth, 1, 3 * D_MODEL))


def _rope_slab(p, kind, tabs):
    if kind == "b":
        cos, sin_s = tabs["b"]
        return p * cos[...] + pltpu.roll(p, HD_B // 2, axis=1) * sin_s[...]
    cos, sin_lo, sin_hi = tabs[kind]
    return (p * cos[...] + pltpu.roll(p, LANES - HD_A // 2, axis=1) * sin_lo[...]
            + pltpu.roll(p, HD_A // 2, axis=1) * sin_hi[...])


def _inproj_kernel(x_ref, sc_ref, sh_ref, g_ref, w_ref,
                   ca_ref, sal_ref, sah_ref, ck_ref, skl_ref, skh_ref, cb_ref, sbs_ref, *out_refs):
    x = x_ref[0]
    ms = jnp.mean(x * x, axis=-1, keepdims=True)
    y = x * lax.rsqrt(ms + EPS) * g_ref[...]
    h = y * sc_ref[0] + sh_ref[0]
    hb = h.astype(BF16)
    tabs = {"a": (ca_ref, sal_ref, sah_ref), "k": (ck_ref, skl_ref, skh_ref), "b": (cb_ref, sbs_ref)}
    c0 = 0
    for (name, width, kind), o_ref in zip(_SEGS, out_refs):
        step = 2 * LANES if width % (2 * LANES) == 0 else LANES
        for off in range(0, width, step):
            p = jnp.dot(hb, w_ref[:, c0 + off:c0 + off + step], preferred_element_type=F32)
            for s in range(step // LANES):
                ps = p[:, s * LANES:(s + 1) * LANES]
                if kind is not None:
                    ps = _rope_slab(ps, kind, tabs)
                if name == "kb":
                    ps = ps * (HD_B ** -0.5)
                if name in ("ga", "gb"):
                    ps = _silu(ps)
                o_ref[0, :, off + s * LANES:off + (s + 1) * LANES] = ps.astype(o_ref.dtype)
        c0 += width


def _prep_w_in(w):
    sizes = (W_A, N_KV_A * HD_A, N_KV_A * HD_A, W_A, N_IDX * IDX_DIM, IDX_DIM, N_IDX, W_B, W_B, W_B, W_B)
    parts = []
    o = 0
    for n in sizes:
        parts.append(w[:, o:o + n])
        o += n
    q_a, k_a, v_a, g_a, q_i, k_i, w_i, q_b, k_b, v_b, g_b = parts
    kk = w.shape[0]
    q_a = q_a * ((HD_A ** -0.5) * LOG2E)
    kiw = jnp.concatenate([k_i, w_i, jnp.zeros((kk, LANES - IDX_DIM - N_IDX), w.dtype)], axis=1)
    segs = {"qa": q_a, "ka": k_a, "va": v_a, "ga": g_a, "qi": q_i, "kiw": kiw,
            "qb": q_b, "kb": k_b, "vb": v_b, "gb": g_b}
    return jnp.concatenate([segs[s[0]] for s in _SEGS], axis=1).astype(BF16)


def _rope_tables(pos):
    posf = pos.astype(F32)[:, None]
    lane = jnp.arange(LANES)[None, :]

    def tab(d):
        half = d // 2
        inv = ROPE_THETA ** (-jnp.arange(half, dtype=F32) * 2.0 / d)
        ang = posf * inv[None, :]
        reps = LANES // half
        return jnp.tile(jnp.cos(ang), (1, reps)), jnp.tile(jnp.sin(ang), (1, reps)), (lane % d) < half

    ca, sa, lo_a = tab(HD_A)
    sal, sah = jnp.where(lo_a, -sa, 0.0), jnp.where(lo_a, 0.0, sa)
    in_k = lane < IDX_DIM
    ck, skl, skh = jnp.where(in_k, ca, 1.0), jnp.where(in_k, sal, 0.0), jnp.where(in_k, sah, 0.0)
    cb, sb, lo_b = tab(HD_B)
    return ca, sal, sah, ck, skl, skh, cb, jnp.where(lo_b, -sb, sb)


def _row_tile(t):
    return min(512, t)


def _inproj(x, scale1p, shift, g_pre, wp, tabs):
    b, t, _ = x.shape
    tm = _row_tile(t)
    nt = t // tm
    const = lambda i, j: (0, 0)
    tab_spec = pl.BlockSpec((tm, LANES), lambda i, j: (j, 0))
    in_specs = [
        pl.BlockSpec((1, tm, D_MODEL), lambda i, j: (i, j, 0)),
        pl.BlockSpec((1, 1, D_MODEL), lambda i, j: (i, 0, 0)),
        pl.BlockSpec((1, 1, D_MODEL), lambda i, j: (i, 0, 0)),
        pl.BlockSpec((1, D_MODEL), const),
        pl.BlockSpec((D_MODEL, _W_COLS), const, pipeline_mode=pl.Buffered(1)),
    ] + [tab_spec] * len(tabs)
    out_specs = [pl.BlockSpec((1, tm, s[1]), lambda i, j: (i, j, 0)) for s in _SEGS]
    out_shape = [jax.ShapeDtypeStruct((b, t, s[1]), _OUT_DTYPES[s[0]]) for s in _SEGS]
    outs = pl.pallas_call(
        _inproj_kernel,
        grid=(b, nt),
        in_specs=in_specs,
        out_specs=out_specs,
        out_shape=out_shape,
        compiler_params=pltpu.CompilerParams(dimension_semantics=("arbitrary", "arbitrary"),
                                             vmem_limit_bytes=VMEM_LIMIT),
        name="inproj",
    )(x, scale1p, shift, g_pre.reshape(1, D_MODEL), wp, *tabs)
    return dict(zip([s[0] for s in _SEGS], outs))


def _attn_kernel(qa_ref, qi_ref, wq_ref, kk_ref, kiw_ref, vt_ref, tri_ref, o_ref,
                 keys_ref, planes_ref, alive_ref, qt_ref, qit_ref, m_ref, acc_ref, sb_ref, x_ref,
                 *, pos_off, topk, s_real):
    j = pl.program_id(1)
    npair = N_HEADS_A // 2

    qlane = lax.broadcasted_iota(I32, (1, QB), 1)
    pos = pos_off + j * QB + qlane
    lim = jnp.minimum((lax.shift_right_logical(pos, 6) + 1) * CHUNK, s_real)
    last_pos = pos_off + (j + 1) * QB - 1
    lim_max = jnp.minimum((lax.shift_right_logical(last_pos, 6) + 1) * CHUNK, s_real)
    n_sel = lax.div(lim_max + (SEL_BLK - 1), SEL_BLK)
    n_att = n_sel * (SEL_BLK // ATT_BLK)

    zeros_half = jnp.zeros((HD_A, QB), BF16)
    for s in range(N_HEADS_A // 2):
        t = qa_ref[0, :, s * LANES:(s + 1) * LANES].astype(F32).T.astype(BF16)
        n = (2 * s) // N_GROUP_A
        for c in range(2):
            cols = slice((2 * s + c) * LANES, (2 * s + c + 1) * LANES)
            qt_ref[n * HD_A:(n + 1) * HD_A, cols] = t[c * HD_A:(c + 1) * HD_A, :]
            qt_ref[(1 - n) * HD_A:(2 - n) * HD_A, cols] = zeros_half
    for s in range(N_IDX // 2):
        t = qi_ref[0, :, s * LANES:(s + 1) * LANES].astype(F32).T.astype(BF16)
        for c in range(2):
            cols = slice((2 * s + c) * LANES, (2 * s + c + 1) * LANES)
            qit_ref[:IDX_DIM, cols] = t[c * IDX_DIM:(c + 1) * IDX_DIM, :]
            qit_ref[IDX_DIM:, cols] = zeros_half
    wt = wq_ref[0].T
    w_idx = [wt[IDX_DIM + h:IDX_DIM + h + 1, :] * IDX_SCALE for h in range(N_IDX)]

    n_blk = n_sel * (SEL_BLK // ATT_BLK)
    vregs = ATT_BLK // 8

    def score_dots(blk, slot):
        kblk = kiw_ref[0, pl.ds(pl.multiple_of(blk * ATT_BLK, ATT_BLK), ATT_BLK), :]
        for m in range(N_IDX // 2):
            x = jnp.dot(kblk, qit_ref[:, 2 * m * LANES:(2 * m + 2) * LANES], preferred_element_type=F32)
            x_ref[slot, 2 * m] = x[:, :LANES]
            x_ref[slot, 2 * m + 1] = x[:, LANES:]

    def score_keys(blk, slot):
        off = pl.multiple_of(blk * ATT_BLK, ATT_BLK)
        sc = None
        for h in range(N_IDX):
            term = jnp.maximum(x_ref[slot, h], 0.0) * w_idx[h]
            sc = term if sc is None else sc + term
        bits = pltpu.bitcast(sc, I32)
        key = bits ^ (lax.shift_right_arithmetic(bits, 31) & 0x7FFFFFFF)
        kpos = off + lax.broadcasted_iota(I32, (ATT_BLK, QB), 0)
        key = jnp.where(kpos < lim, key, INT_MIN)
        keys_ref[pl.ds(off, ATT_BLK), :] = key
        a = [key[8 * r:8 * r + 8, :] ^ INT_MIN for r in range(vregs)]
        for d, msk in ((16, 0x0000FFFF), (8, 0x00FF00FF), (4, 0x0F0F0F0F), (2, 0x33333333), (1, 0x55555555)):
            for k in range(vregs):
                if k & d == 0:
                    t = (a[k] ^ lax.shift_right_logical(a[k + d], d)) & msk
                    a[k] = a[k] ^ t
                    a[k + d] = a[k + d] ^ lax.shift_left(t, d)
        for r in range(vregs):
            planes_ref[r, blk] = a[r]

    x_ref[1] = jnp.zeros(x_ref.shape[1:], F32)

    def score_two_blocks(i, carry):
        score_dots(2 * i, 0)
        score_keys(jnp.maximum(2 * i - 1, 0), 1)
        score_dots(2 * i + 1, 1)
        score_keys(2 * i, 0)
        return carry

    lax.fori_loop(0, n_sel, score_two_blocks, 0)
    score_keys(n_blk - 1, 1)

    alive_ref[...] = jnp.full(alive_ref.shape, -1, I32)

    n_grp = lax.div(n_blk + (SEL_GRP - 1), SEL_GRP)

    def zero_blk(i, carry):
        for r in range(vregs):
            planes_ref[r, i] = jnp.zeros((8, QB), I32)
        return carry

    lax.fori_loop(n_blk, n_grp * SEL_GRP, zero_blk, 0)

    def count_ones(p_cur, p_prev, flip):
        def grp(i, acc):
            blks = pl.ds(pl.multiple_of(i * SEL_GRP, SEL_GRP), SEL_GRP)
            alive = alive_ref[blks]
            if p_prev is not None:
                alive = alive & (planes_ref[p_prev, blks] ^ flip[None])
                alive_ref[blks] = alive
            return acc + _fold(jnp.add, lax.population_count(alive & planes_ref[p_cur, blks]))

        acc = lax.fori_loop(0, n_grp, grp, jnp.zeros((8, QB), I32))
        return jnp.sum(acc, axis=0, keepdims=True)

    def decide(p, ones, st):
        thr_u, rem, _ = st
        take = ones >= rem
        bit = lax.shift_left(jnp.int32(1), 31 - p)
        return (jnp.where(take, thr_u | bit, thr_u), jnp.where(take, rem, rem - ones),
                jnp.where(take, 0, -1))

    zero = jnp.zeros((1, QB), I32)
    st = decide(0, count_ones(0, None, None), (zero, zero + topk, zero))
    st = lax.fori_loop(1, 32, lambda p, s: decide(p, count_ones(p, p - 1, s[2]), s), st)
    thr_u, need, flip = st

    def eq_grp(i, acc):
        blks = pl.ds(pl.multiple_of(i * SEL_GRP, SEL_GRP), SEL_GRP)
        return acc + _fold(jnp.add, lax.population_count(alive_ref[blks] & (planes_ref[31, blks] ^ flip[None])))

    n_eq = jnp.sum(lax.fori_loop(0, n_grp, eq_grp, jnp.zeros((8, QB), I32)), axis=0, keepdims=True)
    thr = jnp.maximum(thr_u ^ INT_MIN, INT_MIN + 1)
    tie_cut = jnp.max(jnp.where((n_eq > need) & (thr_u != 0), 1, 0)) > 0

    @pl.when(tie_cut)
    def _():
        need_f = need.astype(F32)

        def blk(i, run):
            off = pl.multiple_of(i * ATT_BLK, ATT_BLK)
            key = keys_ref[pl.ds(off, ATT_BLK), :]
            eq = key == thr
            pre = jnp.dot(tri_ref[...], jnp.where(eq, 1.0, 0.0).astype(BF16), preferred_element_type=F32)
            keep_eq = (pre + run) <= need_f
            keys_ref[pl.ds(off, ATT_BLK), :] = jnp.where(eq, jnp.where(keep_eq, key, INT_MIN), key)
            return run + pre[ATT_BLK - 1:ATT_BLK, :]

        lax.fori_loop(0, n_att, blk, jnp.zeros((1, QB), F32))

    m_ref[...] = jnp.full(m_ref.shape, NEG, F32)
    acc_ref[...] = jnp.zeros(acc_ref.shape, F32)

    def stage_logits(off, slot):
        kblk = kk_ref[0, pl.ds(off, ATT_BLK), :]
        bias = jnp.where(keys_ref[pl.ds(off, ATT_BLK), :] >= thr, 0.0, NEG).astype(BF16)
        bias2 = jnp.concatenate([bias, bias], axis=1)
        parts = []
        for m in range(npair):
            s = jnp.dot(kblk, qt_ref[:, 2 * m * LANES:(2 * m + 2) * LANES], preferred_element_type=F32)
            sb = s.astype(BF16) + bias2
            sb_ref[slot, m] = sb
            parts.append(_fold(jnp.maximum, sb.reshape(ATT_BLK // PACK16, PACK16, 2 * QB)))
        return parts

    def stage_probs(parts, slot):
        alphas = []
        for m in range(npair):
            m_prev = m_ref[m]
            m_new = jnp.maximum(m_prev, jnp.max(parts[m].astype(F32), axis=0, keepdims=True))
            alphas.append(jnp.exp2(m_prev - m_new))
            sb_ref[slot, m] = jnp.exp2(sb_ref[slot, m] - m_new.astype(BF16))
            m_ref[m] = m_new
        return alphas

    def stage_values(off, slot, alphas):
        for m in range(npair):
            n = (2 * m) // N_GROUP_A
            pv = jnp.dot(vt_ref[0, n * VROWS:(n + 1) * VROWS, pl.ds(off, ATT_BLK)], sb_ref[slot, m],
                         preferred_element_type=F32)
            acc_ref[m] = alphas[m] * acc_ref[m] + pv

    sb_ref[1] = jnp.full(sb_ref.shape[1:], -jnp.inf, BF16)

    def attn_two_blocks(i, parts_prev):
        off0 = pl.multiple_of(i * SEL_BLK, SEL_BLK)
        off1 = pl.multiple_of(off0 + ATT_BLK, ATT_BLK)
        off_prev = pl.multiple_of(jnp.maximum(off0 - ATT_BLK, 0), ATT_BLK)
        parts0 = stage_logits(off0, 0)
        alphas_prev = stage_probs(parts_prev, 1)
        stage_values(off_prev, 1, alphas_prev)
        alphas0 = stage_probs(parts0, 0)
        parts1 = stage_logits(off1, 1)
        stage_values(off0, 0, alphas0)
        return tuple(parts1)

    neg_parts = tuple(jnp.full((PACK16, 2 * QB), NEG, BF16) for _ in range(npair))
    parts_last = lax.fori_loop(0, n_sel, attn_two_blocks, neg_parts)
    alphas_last = stage_probs(parts_last, 1)
    stage_values(pl.multiple_of(n_sel * SEL_BLK - ATT_BLK, ATT_BLK), 1, alphas_last)

    heads = []
    for m in range(npair):
        a = acc_ref[m]
        o = a[:HD_A, :] / a[HD_A:HD_A + 1, :]
        heads += [o[:, :QB], o[:, QB:]]
    o_ref[0] = jnp.concatenate(heads, axis=0).T.astype(o_ref.dtype)


def _attention(qa, qi, kiw_q, kk, kiw_k, vt, pos_off, s_real):
    b, t, _ = qa.shape
    s_pad = kk.shape[1]
    topk = min(MAX_TOPK, s_real // 4)
    idx = jnp.arange(ATT_BLK)
    tri = (idx[None, :] <= idx[:, None]).astype(BF16)
    kern = functools.partial(_attn_kernel, pos_off=pos_off, topk=topk, s_real=s_real)
    npair = N_HEADS_A // 2
    nblk_pad = -(-(s_pad // ATT_BLK) // SEL_GRP) * SEL_GRP
    return pl.pallas_call(
        kern,
        grid=(b, t // QB),
        in_specs=[
            pl.BlockSpec((1, QB, W_A), lambda i, j: (i, j, 0)),
            pl.BlockSpec((1, QB, N_IDX * IDX_DIM), lambda i, j: (i, j, 0)),
            pl.BlockSpec((1, QB, LANES), lambda i, j: (i, j, 0)),
            pl.BlockSpec((1, s_pad, LANES), lambda i, j: (i, 0, 0)),
            pl.BlockSpec((1, s_pad, LANES), lambda i, j: (i, 0, 0)),
            pl.BlockSpec((1, N_KV_A * VROWS, s_pad), lambda i, j: (i, 0, 0)),
            pl.BlockSpec((ATT_BLK, ATT_BLK), lambda i, j: (0, 0)),
        ],
        out_specs=pl.BlockSpec((1, QB, W_A), lambda i, j: (i, j, 0)),
        out_shape=jax.ShapeDtypeStruct((b, t, W_A), BF16),
        scratch_shapes=[
            pltpu.VMEM((s_pad, QB), I32),
            pltpu.VMEM((ATT_BLK // 8, nblk_pad, 8, QB), I32),
            pltpu.VMEM((nblk_pad, 8, QB), I32),
            pltpu.VMEM((LANES, N_HEADS_A * LANES), BF16),
            pltpu.VMEM((LANES, N_IDX * LANES), BF16),
            pltpu.VMEM((npair, 1, 2 * QB), F32),
            pltpu.VMEM((npair, VROWS, 2 * QB), F32),
            pltpu.VMEM((2, npair, ATT_BLK, 2 * QB), BF16),
            pltpu.VMEM((2, N_IDX, ATT_BLK, QB), F32),
        ],
        compiler_params=pltpu.CompilerParams(dimension_semantics=("arbitrary", "arbitrary"),
                                             vmem_limit_bytes=VMEM_LIMIT),
        name="dsa_attention",
    )(qa, qi, kiw_q, kk, kiw_k, vt, tri)


def _attn_operands(ka, va, kiw, s_pad):
    b, s, _ = ka.shape
    pad = ((0, 0), (0, s_pad - s), (0, 0))
    kk = jnp.pad(ka.astype(BF16), pad)
    kiw_k = jnp.pad(kiw.astype(BF16), pad)
    vb = jnp.pad(va.astype(BF16), pad).reshape(b, s_pad, N_KV_A, HD_A)
    ones = jnp.ones((b, s_pad, N_KV_A, VROWS - HD_A), BF16)
    vaug = jnp.concatenate([vb, ones], axis=-1).reshape(b, s_pad, N_KV_A * VROWS)
    return kk, kiw_k, jnp.swapaxes(vaug, 1, 2)


def _pad_rows(a, t_pad):
    return jnp.pad(a, ((0, 0), (0, t_pad - a.shape[1]), (0, 0)))


def _ret_kernel(q_ref, k_ref, v_ref, s0_ref, dm_ref, qd_ref, kd_ref, sd_ref, o_ref, sn_ref, s_scr,
                *, rt, rc):
    t = pl.program_id(1)

    @pl.when(t == 0)
    def _():
        s_scr[...] = s0_ref[0]

    heads = range(N_HEADS_B)
    nt_dims = (((1,), (1,)), ((), ()))
    tn_dims = (((0,), (0,)), ((), ()))
    for c in range(rt // rc):
        rows = slice(c * rc, (c + 1) * rc)
        q = [q_ref[0, rows, h * HD_B:(h + 1) * HD_B] for h in heads]
        k = [k_ref[0, rows, h * HD_B:(h + 1) * HD_B] for h in heads]
        v = [v_ref[0, rows, h * HD_B:(h + 1) * HD_B] for h in heads]
        s = [s_scr[h] for h in heads]
        att = [lax.dot_general(q[h], k[h], nt_dims, preferred_element_type=F32) * dm_ref[h] for h in heads]
        qs = [jnp.dot(q[h], s[h].astype(BF16), preferred_element_type=F32) * qd_ref[h] for h in heads]
        kd = [(k[h].astype(F32) * kd_ref[h]).astype(BF16) for h in heads]
        kv = [lax.dot_general(kd[h], v[h], tn_dims, preferred_element_type=F32) for h in heads]
        o = [jnp.dot(att[h].astype(BF16), v[h], preferred_element_type=F32) + qs[h] for h in heads]
        for h in heads:
            s_scr[h] = s[h] * sd_ref[h] + kv[h]
            mu = jnp.mean(o[h], axis=-1, keepdims=True)
            d = o[h] - mu
            var = jnp.mean(d * d, axis=-1, keepdims=True)
            o_ref[0, rows, h * HD_B:(h + 1) * HD_B] = (d * lax.rsqrt(var + EPS)).astype(o_ref.dtype)

    @pl.when(t == pl.num_programs(1) - 1)
    def _():
        sn_ref[0] = s_scr[...]


def _retention(qb_, kb_, vb_, s0):
    b, t, _ = qb_.shape
    rt = min(512, t)
    rc = min(128, t)
    lg = jnp.log1p(-jnp.exp2(-5.0 - jnp.arange(N_HEADS_B, dtype=F32)))
    i = jnp.arange(rc, dtype=F32)
    diff = i[:, None] - i[None, :]
    dmask = jnp.exp(jnp.where(diff >= 0, diff[None] * lg[:, None, None], -jnp.inf))
    ones = jnp.ones((1, 1, HD_B), F32)
    q_dec = jnp.exp((i + 1.0)[None, :] * lg[:, None])[:, :, None] * ones
    k_dec = jnp.exp((rc - 1.0 - i)[None, :] * lg[:, None])[:, :, None] * ones
    s_dec = jnp.exp(rc * lg)[:, None, None] * jnp.ones((1, HD_B, HD_B), F32)
    blk = pl.BlockSpec((1, rt, W_B), lambda i_, j: (i_, j, 0))
    st = pl.BlockSpec((1, N_HEADS_B, HD_B, HD_B), lambda i_, j: (i_, 0, 0, 0))
    c3 = lambda i_, j: (0, 0, 0)
    kern = functools.partial(_ret_kernel, rt=rt, rc=rc)
    return pl.pallas_call(
        kern,
        grid=(b, t // rt),
        in_specs=[blk, blk, blk, st,
                  pl.BlockSpec((N_HEADS_B, rc, rc), c3),
                  pl.BlockSpec((N_HEADS_B, rc, HD_B), c3),
                  pl.BlockSpec((N_HEADS_B, rc, HD_B), c3),
                  pl.BlockSpec((N_HEADS_B, HD_B, HD_B), c3)],
        out_specs=[blk, st],
        out_shape=[jax.ShapeDtypeStruct((b, t, W_B), BF16),
                   jax.ShapeDtypeStruct((b, N_HEADS_B, HD_B, HD_B), F32)],
        scratch_shapes=[pltpu.VMEM((N_HEADS_B, HD_B, HD_B), F32)],
        compiler_params=pltpu.CompilerParams(dimension_semantics=("arbitrary", "arbitrary"),
                                             vmem_limit_bytes=VMEM_LIMIT),
        name="retention",
    )(qb_, kb_, vb_, s0, dmask, q_dec, k_dec, s_dec)


def _outproj_kernel(x_ref, oa_ref, ga_ref, ob_ref, gb_ref, w_ref, g_ref, gate_ref, o_ref):
    ma = (oa_ref[0].astype(F32) * ga_ref[0].astype(F32)).astype(BF16)
    mb = (ob_ref[0].astype(F32) * gb_ref[0].astype(F32)).astype(BF16)
    y = (jnp.dot(ma, w_ref[:W_A, :], preferred_element_type=F32)
         + jnp.dot(mb, w_ref[W_A:, :], preferred_element_type=F32))
    ms = jnp.mean(y * y, axis=-1, keepdims=True)
    yn = y * lax.rsqrt(ms + EPS) * g_ref[...]
    o_ref[0] = x_ref[0] + gate_ref[0] * yn


def _outproj(x, oa, ga, ob, gb, w_out, g_post, gate):
    b, t, _ = x.shape
    tm = _row_tile(t)
    row = lambda w: pl.BlockSpec((1, tm, w), lambda i, j: (i, j, 0))
    return pl.pallas_call(
        _outproj_kernel,
        grid=(b, t // tm),
        in_specs=[row(D_MODEL), row(W_A), row(W_A), row(W_B), row(W_B),
                  pl.BlockSpec((D_MODEL, D_MODEL), lambda i, j: (0, 0)),
                  pl.BlockSpec((1, D_MODEL), lambda i, j: (0, 0)),
                  pl.BlockSpec((1, 1, D_MODEL), lambda i, j: (i, 0, 0))],
        out_specs=row(D_MODEL),
        out_shape=jax.ShapeDtypeStruct((b, t, D_MODEL), F32),
        compiler_params=pltpu.CompilerParams(dimension_semantics=("arbitrary", "arbitrary"),
                                             vmem_limit_bytes=VMEM_LIMIT),
        name="outproj",
    )(x, oa, ga, ob, gb, w_out.astype(BF16), g_post.reshape(1, D_MODEL), gate)


def _layer(x, mod, pos_off, tabs, wp, g_pre, w_out, g_post, past):
    b, t, _ = x.shape
    shift, scale, gate = jnp.split(mod, 3, axis=-1)
    p = _inproj(x, (1.0 + scale)[:, None, :], shift[:, None, :], g_pre, wp, tabs)
    ka, va, kiw = p["ka"], p["va"], p["kiw"]
    if past is None:
        k_all, v_all, kiw_all = ka, va, kiw
        s0 = jnp.zeros((b, N_HEADS_B, HD_B, HD_B), F32)
    else:
        pk, pv, pki, s0 = past
        plen = pk.shape[1]
        k_all = jnp.concatenate([pk.reshape(b, plen, N_KV_A * HD_A), ka], axis=1)
        v_all = jnp.concatenate([pv.reshape(b, plen, N_KV_A * HD_A), va], axis=1)
        pki = jnp.pad(pki, ((0, 0), (0, 0), (0, LANES - IDX_DIM)))
        kiw_all = jnp.concatenate([pki, kiw], axis=1)
    s_real = k_all.shape[1]
    s_pad = -(-s_real // SEL_BLK) * SEL_BLK
    t_pad = -(-t // QB) * QB
    kk, kiw_k, vt = _attn_operands(k_all, v_all, kiw_all, s_pad)
    oa = _attention(_pad_rows(p["qa"], t_pad), _pad_rows(p["qi"], t_pad), _pad_rows(kiw, t_pad),
                    kk, kiw_k, vt, pos_off, s_real)[:, :t]
    ob, s_new = _retention(p["qb"], p["kb"], p["vb"], s0)
    x_new = _outproj(x, oa, p["ga"], ob, p["gb"], w_out, g_post, gate[:, None, :])
    k_n = ka.reshape(b, t, N_KV_A, HD_A)
    v_n = va.reshape(b, t, N_KV_A, HD_A)
    return x_new, (k_n, v_n, kiw[..., :IDX_DIM], s_new)


def kernel(x_prompt, x_sample, cache_k_a, cache_v_a, cache_kidx_a, state_s_b, c_prompt, c_sample,
           w_ada, b_ada, g_pre, w_in, w_out, g_post):
    depth = w_in.shape[0]
    nb = x_prompt.shape[0]
    past_len = cache_k_a.shape[2]
    tabs_p = _rope_tables(jnp.arange(x_prompt.shape[1], dtype=jnp.int32))
    tabs_s = _rope_tables(past_len + jnp.arange(x_sample.shape[1], dtype=jnp.int32))
    mod = _modulation(jnp.concatenate([c_prompt, c_sample], axis=0), w_ada, b_ada)
    hp, hs = x_prompt, x_sample
    outs_p, outs_s = [], []
    for l in range(depth):
        wp = _prep_w_in(w_in[l])
        hp, new_p = _layer(hp, mod[l, :nb], 0, tabs_p, wp, g_pre[l], w_out[l], g_post[l], None)
        outs_p.append(new_p)
        hs, new_s = _layer(hs, mod[l, nb:], past_len, tabs_s, wp, g_pre[l], w_out[l], g_post[l],
                           (cache_k_a[l], cache_v_a[l], cache_kidx_a[l], state_s_b[l]))
        outs_s.append(new_s)
    stack = lambda outs, i: jnp.stack([o[i] for o in outs])
    return (hp, hs,
            stack(outs_p, 0), stack(outs_p, 1), stack(outs_p, 2), stack(outs_p, 3),
            stack(outs_s, 0), stack(outs_s, 1), stack(outs_s, 2), stack(outs_s, 3))
```
